```python
import math
import jax, jax.numpy as jnp
from jax import lax
import numpy as np

D_MODEL = 1024
BATCH = 4
SEQ = 4096
DEPTH = 2
DEC_BATCH = 32
DEC_SEQ = 8
PAST_LEN = 8192
PAGE_SIZE = 128

N_HEADS = 8
HEAD_DIM = 64
ROT_DIM = HEAD_DIM // 4
ROPE_THETA = 500000.0
CONV_WIDTH = 3
D_FF = -(-8 * D_MODEL // (3 * 256)) * 256
Q_BLOCK = 128
LN_EPS = 1e-5
SUBLN_EPS = 1e-5
ALPHA = (2 * DEPTH) ** 0.25
BETA = (8 * DEPTH) ** -0.25
SCALE = HEAD_DIM ** -0.5
ATTN_LAYER = 1

kernel_name = "hybrid_shortconv_diffattn_decoder_step"


def _lambda_init(layer):
    return 0.8 - 0.6 * math.exp(-0.3 * layer)


def _layernorm(x, g, b):
    xf = x.astype(jnp.float32)
    mu = jnp.mean(xf, axis=-1, keepdims=True)
    xc = xf - mu
    var = jnp.mean(xc * xc, axis=-1, keepdims=True)
    return (xc * lax.rsqrt(var + LN_EPS) * g.astype(jnp.float32) + b.astype(jnp.float32)).astype(x.dtype)


def _short_conv(x, buf, w_in, w_taps, w_out):
    S = x.shape[1]
    gb, gc, h = jnp.split(x @ w_in, 3, axis=-1)
    u = gc * h
    u_pad = jnp.concatenate([buf.astype(u.dtype), u], axis=1)
    conv = (w_taps[0] * u_pad[:, 0:S] + w_taps[1] * u_pad[:, 1:S + 1]
            + w_taps[2] * u_pad[:, 2:S + 2])
    y = (gb * conv) @ w_out
    return y, u_pad[:, -(CONV_WIDTH - 1):]


def _rope(x, pos):
    half = ROT_DIM // 2
    inv = jnp.power(ROPE_THETA, -jnp.arange(0, ROT_DIM, 2, dtype=jnp.float32) / ROT_DIM)
    ang = pos.astype(jnp.float32)[:, None] * inv[None, :]
    cos = jnp.cos(ang)[None, :, None, :]
    sin = jnp.sin(ang)[None, :, None, :]
    xr = x[..., :ROT_DIM].astype(jnp.float32)
    x1, x2 = xr[..., :half], xr[..., half:]
    rot = jnp.concatenate([x1 * cos - x2 * sin, x2 * cos + x1 * sin], axis=-1).astype(x.dtype)
    return jnp.concatenate([rot, x[..., ROT_DIM:]], axis=-1)


def _qkv(x, w_qkv, pos):
    B, S, _ = x.shape
    q, k, v = jnp.split(x @ w_qkv, 3, axis=-1)
    q = q.reshape(B, S, 2 * N_HEADS, HEAD_DIM)
    k = k.reshape(B, S, 2 * N_HEADS, HEAD_DIM)
    v = v.reshape(B, S, N_HEADS, 2 * HEAD_DIM)
    return _rope(q, pos), _rope(k, pos), v


def _diff_core(q, k, v, mask, lam):
    s = jnp.einsum('bqhd,bkhd->bhqk', q, k).astype(jnp.float32) * SCALE
    s = jnp.where(mask[None, None], s, jnp.finfo(jnp.float32).min)
    p = jax.nn.softmax(s, axis=-1)
    B, _, Sq, Sk = p.shape
    p = p.reshape(B, N_HEADS, 2, Sq, Sk)
    a = p[:, :, 0] - lam * p[:, :, 1]
    return jnp.einsum('bhqk,bkhe->bqhe', a, v.astype(jnp.float32))


def _prompt_attn(q, k, v, lam):
    B, S = q.shape[:2]
    nb = S // Q_BLOCK
    qb = q.reshape(B, nb, Q_BLOCK, 2 * N_HEADS, HEAD_DIM).swapaxes(0, 1)
    k_pos = jnp.arange(S)

    def one_block(args):
        q_blk, b_idx = args
        q_pos = b_idx * Q_BLOCK + jnp.arange(Q_BLOCK)
        return _diff_core(q_blk, k, v, k_pos[None, :] <= q_pos[:, None], lam)

    o = lax.map(one_block, (qb, jnp.arange(nb)))
    return o.swapaxes(0, 1).reshape(B, S, N_HEADS, 2 * HEAD_DIM)


def _sample_attn(q, k, v, cache_k, cache_v, page_table, lam):
    DB, n_pages = page_table.shape
    past_len = n_pages * cache_k.shape[1]
    DS = q.shape[1]
    past_k = cache_k[page_table].reshape(DB, past_len, 2 * N_HEADS, HEAD_DIM)
    past_v = cache_v[page_table].reshape(DB, past_len, N_HEADS, 2 * HEAD_DIM)
    k_all = jnp.concatenate([past_k, k.astype(past_k.dtype)], axis=1)
    v_all = jnp.concatenate([past_v, v.astype(past_v.dtype)], axis=1)
    k_pos = jnp.arange(past_len + DS)
    q_pos = past_len + jnp.arange(DS)
    return _diff_core(q, k_all, v_all, k_pos[None, :] <= q_pos[:, None], lam)


def _attn_out(o, subln_g, lam_init, w_out, dtype):
    o = o * lax.rsqrt(jnp.mean(o * o, axis=-1, keepdims=True) + SUBLN_EPS)
    o = o * subln_g.astype(jnp.float32) * (1.0 - lam_init)
    B, S = o.shape[:2]
    return o.reshape(B, S, N_HEADS * 2 * HEAD_DIM).astype(dtype) @ w_out


def _swiglu(x, w_in, w_out):
    gate, up = jnp.split(x @ w_in, 2, axis=-1)
    return (jax.nn.silu(gate) * up) @ w_out


def setup_inputs(seed: int = 0) -> dict:
    key = jax.random.key(seed)
    ks = jax.random.split(key, 24)
    D, F, f32 = D_MODEL, D_FF, jnp.float32
    n_pages = PAST_LEN // PAGE_SIZE
    n_used = DEC_BATCH * n_pages
    n_pool = n_used + max(1, n_used // 4)
    nrm = lambda k, shp: jax.random.normal(k, shp, f32)
    page_table = jax.random.permutation(ks[5], n_pool)[:n_used].astype(jnp.int32).reshape(DEC_BATCH, n_pages)
    return {
        "x_prompt": nrm(ks[0], (BATCH, SEQ, D)),
        "x_sample": nrm(ks[1], (DEC_BATCH, DEC_SEQ, D)),
        "state_conv": nrm(ks[2], (DEC_BATCH, CONV_WIDTH - 1, D)),
        "cache_k": nrm(ks[3], (n_pool, PAGE_SIZE, 2 * N_HEADS, HEAD_DIM)),
        "cache_v": nrm(ks[4], (n_pool, PAGE_SIZE, N_HEADS, 2 * HEAD_DIM)),
        "page_table": page_table,
        "w_conv_in": nrm(ks[6], (D, 3 * D)) * D ** -0.5,
        "w_conv": nrm(ks[7], (CONV_WIDTH, D)) * CONV_WIDTH ** -0.5,
        "w_conv_out": nrm(ks[8], (D, D)) * (D ** -0.5 * BETA),
        "w_qkv": nrm(ks[9], (D, 3 * D)) * D ** -0.5,
        "lambda_q1": nrm(ks[10], (HEAD_DIM,)) * 0.1,
        "lambda_k1": nrm(ks[11], (HEAD_DIM,)) * 0.1,
        "lambda_q2": nrm(ks[12], (HEAD_DIM,)) * 0.1,
        "lambda_k2": nrm(ks[13], (HEAD_DIM,)) * 0.1,
        "subln_g": 1.0 + 0.02 * nrm(ks[14], (2 * HEAD_DIM,)),
        "w_attn_out": nrm(ks[15], (D, D)) * (D ** -0.5 * BETA),
        "ln_mix_g": 1.0 + 0.02 * nrm(ks[16], (DEPTH, D)),
        "ln_mix_b": 0.02 * nrm(ks[17], (DEPTH, D)),
        "w_ffn_in": nrm(ks[18], (DEPTH, D, 2 * F)) * D ** -0.5,
        "w_ffn_out": nrm(ks[19], (DEPTH, F, D)) * (F ** -0.5 * BETA),
        "ln_ffn_g": 1.0 + 0.02 * nrm(ks[20], (DEPTH, D)),
        "ln_ffn_b": 0.02 * nrm(ks[21], (DEPTH, D)),
    }


def reference(x_prompt, x_sample, state_conv, cache_k, cache_v, page_table,
              w_conv_in, w_conv, w_conv_out, w_qkv,
              lambda_q1, lambda_k1, lambda_q2, lambda_k2, subln_g, w_attn_out,
              ln_mix_g, ln_mix_b, w_ffn_in, w_ffn_out, ln_ffn_g, ln_ffn_b):
    xp, xs = x_prompt, x_sample
    S, DS = xp.shape[1], xs.shape[1]
    past_len = page_table.shape[1] * cache_k.shape[1]
    pos_p = jnp.arange(S)
    pos_s = past_len + jnp.arange(DS)
    for i in range(DEPTH):
        if i % 2 == 0:
            buf_p = jnp.zeros((xp.shape[0], CONV_WIDTH - 1, xp.shape[2]), xp.dtype)
            mp, conv_p = _short_conv(xp, buf_p, w_conv_in, w_conv, w_conv_out)
            ms, conv_s = _short_conv(xs, state_conv, w_conv_in, w_conv, w_conv_out)
        else:
            lam_init = _lambda_init(i)
            lam = (jnp.exp(jnp.sum(lambda_q1.astype(jnp.float32) * lambda_k1.astype(jnp.float32)))
                   - jnp.exp(jnp.sum(lambda_q2.astype(jnp.float32) * lambda_k2.astype(jnp.float32)))
                   + lam_init)
            qp, k_p, v_p = _qkv(xp, w_qkv, pos_p)
            op = _prompt_attn(qp, k_p, v_p, lam)
            mp = _attn_out(op, subln_g, lam_init, w_attn_out, xp.dtype)
            qs, k_s, v_s = _qkv(xs, w_qkv, pos_s)
            os_ = _sample_attn(qs, k_s, v_s, cache_k, cache_v, page_table, lam)
            ms = _attn_out(os_, subln_g, lam_init, w_attn_out, xs.dtype)
        xp = _layernorm(ALPHA * xp + mp, ln_mix_g[i], ln_mix_b[i])
        xs = _layernorm(ALPHA * xs + ms, ln_mix_g[i], ln_mix_b[i])
        xp = _layernorm(ALPHA * xp + _swiglu(xp, w_ffn_in[i], w_ffn_out[i]), ln_ffn_g[i], ln_ffn_b[i])
        xs = _layernorm(ALPHA * xs + _swiglu(xs, w_ffn_in[i], w_ffn_out[i]), ln_ffn_g[i], ln_ffn_b[i])
    return (xp, xs, conv_p, k_p, v_p, conv_s, k_s, v_s)
```

```python
import functools
import math

import jax
import jax.numpy as jnp
from jax import lax
from jax.experimental import pallas as pl
from jax.experimental.pallas import tpu as pltpu

N_HEADS = 8
HEAD_DIM = 64
V_DIM = 2 * HEAD_DIM
ROT_DIM = HEAD_DIM // 4
ROPE_THETA = 500000.0
CONV_WIDTH = 3
DEPTH = 2
LN_EPS = 1e-5
SUBLN_EPS = 1e-5
ALPHA = (2 * DEPTH) ** 0.25
SCALE = HEAD_DIM ** -0.5
ATTN_LAYER = 1
LAM_INIT = 0.8 - 0.6 * math.exp(-0.3 * ATTN_LAYER)

LANES = 128
SUBLANES = 8
VMEM_LIMIT_BYTES = 56 * 1024 * 1024

TOKEN_TILE = 512
ATTN_Q_TILE = 512
ATTN_KV_TILE = 512
PAGES_PER_STEP = 8

_NT = (((1,), (1,)), ((), ()))


def _bf16(x):
    return x.astype(jnp.bfloat16)


def _layernorm(y, g, b):
    mu = jnp.mean(y, axis=-1, keepdims=True)
    yc = y - mu
    var = jnp.mean(yc * yc, axis=-1, keepdims=True)
    return yc * lax.rsqrt(var + LN_EPS) * g + b


def _const_spec(shape):
    nd = len(shape)
    return pl.BlockSpec(shape, lambda *_: (0,) * nd, pipeline_mode=pl.Buffered(1))


def _params(n_axes):
    return pltpu.CompilerParams(
        dimension_semantics=("arbitrary",) * n_axes,
        vmem_limit_bytes=VMEM_LIMIT_BYTES)


def _conv_prompt_body(x_ref, w_in_ref, taps_ref, w_out_ref, g_ref, b_ref,
                      o_ref, tail_ref, carry_ref, *, tiles_per_seq):
    i = pl.program_id(0)
    x = x_ref[...]
    t, d = x.shape
    h3 = jnp.dot(_bf16(x), w_in_ref[...], preferred_element_type=jnp.float32)
    gb, gc, h = h3[:, :d], h3[:, d:2 * d], h3[:, 2 * d:]
    u = gc * h

    @pl.when(i % tiles_per_seq == 0)
    def _():
        carry_ref[...] = jnp.zeros_like(carry_ref)

    row = lax.broadcasted_iota(jnp.int32, (t, 1), 0)
    c6 = carry_ref[SUBLANES - 2:SUBLANES - 1, :]
    c7 = carry_ref[SUBLANES - 1:SUBLANES, :]
    u1 = jnp.where(row == 0, c7, pltpu.roll(u, 1, 0))
    u2 = jnp.where(row == 0, c6, jnp.where(row == 1, c7, pltpu.roll(u, 2, 0)))
    taps = taps_ref[...]
    conv = taps[0:1, :] * u2 + taps[1:2, :] * u1 + taps[2:3, :] * u
    y = jnp.dot(_bf16(gb * conv), w_out_ref[...], preferred_element_type=jnp.float32)
    o_ref[...] = _layernorm(ALPHA * x + y, g_ref[...], b_ref[...])

    carry_ref[...] = u[t - SUBLANES:, :]

    @pl.when(i % tiles_per_seq == tiles_per_seq - 1)
    def _():
        tail_ref[...] = u[t - SUBLANES:, :]


def _conv_prompt(x, w_in, taps, w_out, g, b, seq_len):
    n, d = x.shape
    tm = TOKEN_TILE
    tiles_per_seq = seq_len // tm
    return pl.pallas_call(
        functools.partial(_conv_prompt_body, tiles_per_seq=tiles_per_seq),
        grid=(n // tm,),
        in_specs=[
            pl.BlockSpec((tm, d), lambda i: (i, 0)),
            _const_spec(w_in.shape), _const_spec(taps.shape), _const_spec(w_out.shape),
            _const_spec(g.shape), _const_spec(b.shape),
        ],
        out_specs=[
            pl.BlockSpec((tm, d), lambda i: (i, 0)),
            pl.BlockSpec((SUBLANES, d), lambda i: (i // tiles_per_seq, 0)),
        ],
        out_shape=[
            jax.ShapeDtypeStruct((n, d), jnp.float32),
            jax.ShapeDtypeStruct((n // seq_len * SUBLANES, d), jnp.float32),
        ],
        scratch_shapes=[pltpu.VMEM((SUBLANES, d), jnp.float32)],
        compiler_params=_params(1),
        name="conv_prompt",
    )(x, w_in, taps, w_out, g, b)


def _conv_sample_body(x_ref, st_ref, w_in_ref, taps_ref, w_out_ref, g_ref, b_ref,
                      o_ref, u_ref):
    x = x_ref[...]
    t, d = x.shape
    h3 = jnp.dot(_bf16(x), w_in_ref[...], preferred_element_type=jnp.float32)
    gb, gc, h = h3[:, :d], h3[:, d:2 * d], h3[:, 2 * d:]
    u = gc * h
    st = st_ref[...]
    pos = lax.broadcasted_iota(jnp.int32, (t, 1), 0) % SUBLANES
    u1 = jnp.where(pos == 0, pltpu.roll(st, t - 1, 0), pltpu.roll(u, 1, 0))
    u2 = jnp.where(pos < 2, st, pltpu.roll(u, 2, 0))
    taps = taps_ref[...]
    conv = taps[0:1, :] * u2 + taps[1:2, :] * u1 + taps[2:3, :] * u
    y = jnp.dot(_bf16(gb * conv), w_out_ref[...], preferred_element_type=jnp.float32)
    o_ref[...] = _layernorm(ALPHA * x + y, g_ref[...], b_ref[...])
    u_ref[...] = u


def _conv_sample(x, st, w_in, taps, w_out, g, b):
    n, d = x.shape
    return pl.pallas_call(
        _conv_sample_body,
        grid=(1,),
        in_specs=[_const_spec(a.shape) for a in (x, st, w_in, taps, w_out, g, b)],
        out_specs=[_const_spec((n, d)), _const_spec((n, d))],
        out_shape=[jax.ShapeDtypeStruct((n, d), jnp.float32)] * 2,
        compiler_params=_params(1),
        name="conv_sample",
    )(x, st, w_in, taps, w_out, g, b)


def _ffn_body(x_ref, w_in_ref, w_out_ref, g_ref, b_ref, o_ref):
    x = x_ref[...]
    f = w_out_ref.shape[0]
    h = jnp.dot(_bf16(x), w_in_ref[...], preferred_element_type=jnp.float32)
    gate, up = h[:, :f], h[:, f:]
    a = gate * jax.nn.sigmoid(gate) * up
    y = jnp.dot(_bf16(a), w_out_ref[...], preferred_element_type=jnp.float32)
    o_ref[...] = _layernorm(ALPHA * x + y, g_ref[...], b_ref[...])


def _ffn(x, w_in, w_out, g, b):
    n, d = x.shape
    tm = min(TOKEN_TILE, n)
    return pl.pallas_call(
        _ffn_body,
        grid=(n // tm,),
        in_specs=[
            pl.BlockSpec((tm, d), lambda i: (i, 0)),
            _const_spec(w_in.shape), _const_spec(w_out.shape),
            _const_spec(g.shape), _const_spec(b.shape),
        ],
        out_specs=pl.BlockSpec((tm, d), lambda i: (i, 0)),
        out_shape=jax.ShapeDtypeStruct((n, d), jnp.float32),
        compiler_params=_params(1),
        name="ffn",
    )(x, w_in, w_out, g, b)


def _rope_tables(pos):
    half = ROT_DIM // 2
    inv = jnp.power(ROPE_THETA, -jnp.arange(0, ROT_DIM, 2, dtype=jnp.float32) / ROT_DIM)
    ang = pos.astype(jnp.float32)[:, None] * inv[None, :]
    cos, sin = jnp.cos(ang), jnp.sin(ang)
    dd = jnp.arange(LANES) % HEAD_DIM
    cos_l = jnp.take(cos, dd % half, axis=1)
    sin_l = jnp.take(sin, dd % half, axis=1)
    c = jnp.where(dd[None, :] < ROT_DIM, cos_l, 1.0)
    s_up = jnp.where(dd[None, :] < half, -sin_l, 0.0)
    s_dn = jnp.where((dd[None, :] >= half) & (dd[None, :] < ROT_DIM), sin_l, 0.0)
    return c, s_up, s_dn


def _rope(x, c, s_up, s_dn):
    half = ROT_DIM // 2
    outs = []
    for g in range(x.shape[1] // LANES):
        xg = x[:, g * LANES:(g + 1) * LANES]
        x_up = pltpu.roll(xg, LANES - half, 1)
        x_dn = pltpu.roll(xg, half, 1)
        outs.append(xg * c + x_up * s_up + x_dn * s_dn)
    return jnp.concatenate(outs, axis=1)


def _qkv_body(x_ref, w_ref, c_ref, su_ref, sd_ref,
              q_ref, k_ref, v_ref, kb_ref, vb_ref):
    x = x_ref[...]
    d = x.shape[1]
    h3 = jnp.dot(_bf16(x), w_ref[...], preferred_element_type=jnp.float32)
    c, su, sd = c_ref[...], su_ref[...], sd_ref[...]
    q = _rope(h3[:, :d], c, su, sd)
    k = _rope(h3[:, d:2 * d], c, su, sd)
    v = h3[:, 2 * d:]
    q_ref[...] = (q * SCALE).astype(q_ref.dtype)
    k_ref[...] = k
    v_ref[...] = v
    kb_ref[...] = _bf16(k)
    vb_ref[...] = _bf16(v)


def _qkv(x, w, tables, seq_tiles, q_dtype):
    n, d = x.shape
    tm = min(TOKEN_TILE, n)
    tspec = pl.BlockSpec((tm, LANES), lambda i: (i % seq_tiles, 0))
    row = pl.BlockSpec((tm, d), lambda i: (i, 0))
    return pl.pallas_call(
        _qkv_body,
        grid=(n // tm,),
        in_specs=[row, _const_spec(w.shape), tspec, tspec, tspec],
        out_specs=[row] * 5,
        out_shape=[
            jax.ShapeDtypeStruct((n, d), q_dtype),
            jax.ShapeDtypeStruct((n, d), jnp.float32),
            jax.ShapeDtypeStruct((n, d), jnp.float32),
            jax.ShapeDtypeStruct((n, d), jnp.bfloat16),
            jax.ShapeDtypeStruct((n, d), jnp.bfloat16),
        ],
        compiler_params=_params(1),
        name="qkv",
    )(x, w, *tables)


def _lambda(lamv_ref):
    lv = lamv_ref[...]
    d1 = jnp.sum(lv[0:1, :] * lv[1:2, :], axis=-1, keepdims=True)
    d2 = jnp.sum(lv[2:3, :] * lv[3:4, :], axis=-1, keepdims=True)
    return jnp.exp(d1) - jnp.exp(d2) + LAM_INIT


def _subln(o, g):
    o = o * lax.rsqrt(jnp.mean(o * o, axis=-1, keepdims=True) + SUBLN_EPS)
    return o * g * (1.0 - LAM_INIT)


def _prompt_attn_body(q_ref, k_ref, v_ref, lamv_ref, g_ref, o_ref,
                      m_ref, l_ref, acc_ref):
    tq, tk = ATTN_Q_TILE, ATTN_KV_TILE
    i = pl.program_id(1)
    lam = _lambda(lamv_ref)
    g = g_ref[...]
    lane = lax.broadcasted_iota(jnp.int32, (1, LANES), 1)
    rows = lax.broadcasted_iota(jnp.int32, (tq, tk), 0)
    cols = lax.broadcasted_iota(jnp.int32, (tq, tk), 1)
    neg = jnp.finfo(jnp.float32).min

    for j in range(N_HEADS):
        cs = slice(j * LANES, (j + 1) * LANES)
        qp = q_ref[0, :, cs]
        q_sub = (jnp.where(lane < HEAD_DIM, qp, 0), jnp.where(lane >= HEAD_DIM, qp, 0))
        m_ref[...] = jnp.full_like(m_ref, neg)
        l_ref[...] = jnp.zeros_like(l_ref)
        acc_ref[...] = jnp.zeros_like(acc_ref)

        def update(kv, masked):
            kblk = k_ref[0, pl.ds(kv * tk, tk), cs]
            vblk = v_ref[0, pl.ds(kv * tk, tk), cs]
            for c in range(2):
                s = lax.dot_general(q_sub[c], kblk, _NT, preferred_element_type=jnp.float32)
                if masked:
                    s = jnp.where(cols <= rows, s, neg)
                m_old = m_ref[c]
                m_new = jnp.maximum(m_old, jnp.max(s, axis=-1, keepdims=True))
                alpha = jnp.exp(m_old - m_new)
                p = jnp.exp(s - m_new)
                l_ref[c] = alpha * l_ref[c] + jnp.sum(p, axis=-1, keepdims=True)
                acc_ref[c] = alpha * acc_ref[c] + jnp.dot(
                    _bf16(p), vblk, preferred_element_type=jnp.float32)
                m_ref[c] = m_new

        def loop_body(kv, carry):
            update(kv, masked=False)
            return carry

        lax.fori_loop(0, i, loop_body, 0)
        update(i, masked=True)

        o = acc_ref[0] / l_ref[0] - lam * (acc_ref[1] / l_ref[1])
        o_ref[0, :, cs] = _subln(o, g).astype(o_ref.dtype)


def _prompt_attn(q, k, v, lamv, g):
    b, s, d = q.shape
    tq = ATTN_Q_TILE
    assert ATTN_Q_TILE == ATTN_KV_TILE and s % tq == 0
    kv_spec = pl.BlockSpec((1, s, d), lambda bi, i: (bi, 0, 0), pipeline_mode=pl.Buffered(1))
    return pl.pallas_call(
        _prompt_attn_body,
        grid=(b, s // tq),
        in_specs=[
            pl.BlockSpec((1, tq, d), lambda bi, i: (bi, i, 0)),
            kv_spec, kv_spec,
            _const_spec(lamv.shape), _const_spec(g.shape),
        ],
        out_specs=pl.BlockSpec((1, tq, d), lambda bi, i: (bi, i, 0)),
        out_shape=jax.ShapeDtypeStruct((b, s, d), jnp.bfloat16),
        scratch_shapes=[
            pltpu.VMEM((2, tq, 1), jnp.float32),
            pltpu.VMEM((2, tq, 1), jnp.float32),
            pltpu.VMEM((2, tq, V_DIM), jnp.float32),
        ],
        compiler_params=_params(2),
        name="prompt_attn",
    )(q, k, v, lamv, g)


def _sample_attn_body(pt_ref, q_ref, kn_ref, vn_ref, lamv_ref, g_ref, *refs,
                      n_steps):
    pps = PAGES_PER_STEP
    k_refs, v_refs = refs[:pps], refs[pps:2 * pps]
    o_ref = refs[2 * pps]
    m_ref, l_ref, acc_ref = refs[2 * pps + 1:]
    c = pl.program_id(1)
    ds = q_ref.shape[2]
    page = k_refs[0].shape[1]
    neg = jnp.finfo(jnp.float32).min
    q3 = _bf16(q_ref[0])

    def scores(k3):
        s = jnp.einsum("hqd,htd->hqt", q3, k3, preferred_element_type=jnp.float32)
        return s.reshape(N_HEADS, 2 * ds, k3.shape[1])

    def weighted(p, v3):
        return jnp.einsum("jrt,jte->jre", _bf16(p), v3, preferred_element_type=jnp.float32)

    @pl.when(c == 0)
    def _():
        kn = kn_ref[0]
        vn = vn_ref[0]
        kn3 = _bf16(jnp.concatenate(
            [kn, jnp.zeros((kn.shape[0], page - ds, kn.shape[2]), jnp.float32)], axis=1))
        vn3 = _bf16(jnp.concatenate(
            [vn, jnp.zeros((vn.shape[0], page - ds, vn.shape[2]), jnp.float32)], axis=1))
        s = scores(kn3)
        qi = lax.broadcasted_iota(jnp.int32, s.shape, 1) % ds
        tt = lax.broadcasted_iota(jnp.int32, s.shape, 2)
        s = jnp.where(tt <= qi, s, neg)
        m = jnp.max(s, axis=-1, keepdims=True)
        p = jnp.exp(s - m)
        m_ref[...] = m
        l_ref[...] = jnp.sum(p, axis=-1, keepdims=True)
        acc_ref[...] = weighted(p, vn3)

    k3 = jnp.concatenate(
        [_bf16(pltpu.einshape("thd->htd", k_refs[pg][0])) for pg in range(pps)], axis=1)
    v3 = jnp.concatenate(
        [_bf16(pltpu.einshape("tje->jte", v_refs[pg][0])) for pg in range(pps)], axis=1)
    s = scores(k3)
    m_old = m_ref[...]
    m_new = jnp.maximum(m_old, jnp.max(s, axis=-1, keepdims=True))
    alpha = jnp.exp(m_old - m_new)
    p = jnp.exp(s - m_new)
    l_ref[...] = alpha * l_ref[...] + jnp.sum(p, axis=-1, keepdims=True)
    acc_ref[...] = alpha * acc_ref[...] + weighted(p, v3)
    m_ref[...] = m_new

    @pl.when(c == n_steps - 1)
    def _():
        lam = _lambda(lamv_ref)
        g = g_ref[...]
        o = acc_ref[...] / l_ref[...]
        o = o[:, :ds, :] - lam * o[:, ds:, :]
        o = _subln(o, g)
        for j in range(N_HEADS):
            o_ref[0, :, j * V_DIM:(j + 1) * V_DIM] = o[j]


def _sample_attn(q3, k_new3, v_new3, cache_k, cache_v, page_table, lamv, g):
    db, nh2, ds, hd = q3.shape
    n_pages = page_table.shape[1]
    page = cache_k.shape[1]
    pps = PAGES_PER_STEP
    n_steps = n_pages // pps
    d = nh2 * hd
    assert n_pages % pps == 0 and ds == SUBLANES

    def page_spec(shape, pg):
        return pl.BlockSpec(
            (1,) + shape[1:],
            lambda b, c, pt: (pt[b * n_pages + c * pps + pg], 0, 0, 0))

    def seq_spec(shape):
        return pl.BlockSpec((1,) + shape[1:], lambda b, c, pt: (b,) + (0,) * (len(shape) - 1))

    grid_spec = pltpu.PrefetchScalarGridSpec(
        num_scalar_prefetch=1,
        grid=(db, n_steps),
        in_specs=[seq_spec(q3.shape), seq_spec(k_new3.shape), seq_spec(v_new3.shape),
                  pl.BlockSpec(lamv.shape, lambda b, c, pt: (0, 0)),
                  pl.BlockSpec(g.shape, lambda b, c, pt: (0, 0))]
                 + [page_spec(cache_k.shape, pg) for pg in range(pps)]
                 + [page_spec(cache_v.shape, pg) for pg in range(pps)],
        out_specs=seq_spec((db, ds, d)),
        scratch_shapes=[
            pltpu.VMEM((N_HEADS, 2 * ds, 1), jnp.float32),
            pltpu.VMEM((N_HEADS, 2 * ds, 1), jnp.float32),
            pltpu.VMEM((N_HEADS, 2 * ds, V_DIM), jnp.float32),
        ],
    )
    return pl.pallas_call(
        functools.partial(_sample_attn_body, n_steps=n_steps),
        grid_spec=grid_spec,
        out_shape=jax.ShapeDtypeStruct((db, ds, d), jnp.float32),
        compiler_params=_params(2),
        name="sample_attn",
    )(page_table.reshape(-1), q3, k_new3, v_new3, lamv, g,
      *([cache_k] * pps), *([cache_v] * pps))


def _proj_ln_body(o_ref, x_ref, w_ref, g_ref, b_ref, y_ref):
    y = jnp.dot(_bf16(o_ref[...]), w_ref[...], preferred_element_type=jnp.float32)
    y_ref[...] = _layernorm(ALPHA * x_ref[...] + y, g_ref[...], b_ref[...])


def _proj_ln(o, x, w, g, b):
    n, d = x.shape
    tm = min(TOKEN_TILE, n)
    row = pl.BlockSpec((tm, d), lambda i: (i, 0))
    return pl.pallas_call(
        _proj_ln_body,
        grid=(n // tm,),
        in_specs=[row, row, _const_spec(w.shape), _const_spec(g.shape), _const_spec(b.shape)],
        out_specs=row,
        out_shape=jax.ShapeDtypeStruct((n, d), jnp.float32),
        compiler_params=_params(1),
        name="proj_ln",
    )(o, x, w, g, b)


def kernel(x_prompt, x_sample, state_conv, cache_k, cache_v, page_table, w_conv_in, w_conv, w_conv_out, w_qkv, lambda_q1, lambda_k1, lambda_q2, lambda_k2, subln_g, w_attn_out, ln_mix_g, ln_mix_b, w_ffn_in, w_ffn_out, ln_ffn_g, ln_ffn_b):
    b, s, d = x_prompt.shape
    db, ds, _ = x_sample.shape
    n_pool, page = cache_k.shape[:2]
    past_len = page_table.shape[1] * page
    f32 = jnp.float32
    assert ds == SUBLANES and s % TOKEN_TILE == 0

    w_conv_in_b, w_conv_out_b = _bf16(w_conv_in), _bf16(w_conv_out)
    w_qkv_b, w_attn_out_b = _bf16(w_qkv), _bf16(w_attn_out)
    w_ffn_in_b, w_ffn_out_b = _bf16(w_ffn_in), _bf16(w_ffn_out)
    taps = w_conv.astype(f32)
    row = lambda a: a.reshape(1, -1).astype(f32)
    lamv = jnp.stack([lambda_q1, lambda_k1, lambda_q2, lambda_k2]).astype(f32)
    g_sub = row(subln_g)

    xp = x_prompt.reshape(b * s, d)
    xs = x_sample.reshape(db * ds, d)

    i = 0
    xp, tail_p = _conv_prompt(xp, w_conv_in_b, taps, w_conv_out_b,
                              row(ln_mix_g[i]), row(ln_mix_b[i]), s)
    conv_p = tail_p.reshape(b, SUBLANES, d)[:, SUBLANES - (CONV_WIDTH - 1):]
    st = jnp.pad(state_conv, ((0, 0), (0, ds - (CONV_WIDTH - 1)), (0, 0))).reshape(db * ds, d)
    xs, u_s = _conv_sample(xs, st, w_conv_in_b, taps, w_conv_out_b,
                           row(ln_mix_g[i]), row(ln_mix_b[i]))
    conv_s = u_s.reshape(db, ds, d)[:, ds - (CONV_WIDTH - 1):]
    xp = _ffn(xp, w_ffn_in_b[i], w_ffn_out_b[i], row(ln_ffn_g[i]), row(ln_ffn_b[i]))
    xs = _ffn(xs, w_ffn_in_b[i], w_ffn_out_b[i], row(ln_ffn_g[i]), row(ln_ffn_b[i]))

    i = 1
    tab_p = _rope_tables(jnp.arange(s))
    tab_s = _rope_tables(past_len + jnp.arange(db * ds) % ds)
    qp, k_p, v_p, kb_p, vb_p = _qkv(xp, w_qkv_b, tab_p, s // TOKEN_TILE, jnp.bfloat16)
    qs, k_s, v_s, _, _ = _qkv(xs, w_qkv_b, tab_s, 1, f32)

    op = _prompt_attn(qp.reshape(b, s, d), kb_p.reshape(b, s, d), vb_p.reshape(b, s, d),
                      lamv, g_sub)
    heads_first = lambda a, nh: a.reshape(db, ds, nh, d // nh).transpose(0, 2, 1, 3)
    os_ = _sample_attn(heads_first(qs, 2 * N_HEADS), heads_first(k_s, 2 * N_HEADS),
                       heads_first(v_s, N_HEADS), cache_k, cache_v, page_table, lamv, g_sub)
    xp = _proj_ln(op.reshape(b * s, d), xp, w_attn_out_b, row(ln_mix_g[i]), row(ln_mix_b[i]))
    xs = _proj_ln(os_.reshape(db * ds, d), xs, w_attn_out_b, row(ln_mix_g[i]), row(ln_mix_b[i]))
    xp = _ffn(xp, w_ffn_in_b[i], w_ffn_out_b[i], row(ln_ffn_g[i]), row(ln_ffn_b[i]))
    xs = _ffn(xs, w_ffn_in_b[i], w_ffn_out_b[i], row(ln_ffn_g[i]), row(ln_ffn_b[i]))

    return (xp.reshape(b, s, d), xs.reshape(db, ds, d), conv_p,
            k_p.reshape(b, s, 2 * N_HEADS, HEAD_DIM), v_p.reshape(b, s, N_HEADS, V_DIM),
            conv_s,
            k_s.reshape(db, ds, 2 * N_HEADS, HEAD_DIM), v_s.reshape(db, ds, N_HEADS, V_DIM))
```

```python
import functools
import math

import jax
import jax.numpy as jnp
from jax import lax
from jax.experimental import pallas as pl
from jax.experimental.pallas import tpu as pltpu

N_HEADS = 8
HEAD_DIM = 64
V_DIM = 2 * HEAD_DIM
ROT_DIM = HEAD_DIM // 4
ROPE_THETA = 500000.0
CONV_WIDTH = 3
DEPTH = 2
LN_EPS = 1e-5
SUBLN_EPS = 1e-5
ALPHA = (2 * DEPTH) ** 0.25
SCALE = HEAD_DIM ** -0.5
LOG2_E = math.log2(math.e)
ATTN_LAYER = 1
LAM_INIT = 0.8 - 0.6 * math.exp(-0.3 * ATTN_LAYER)

LANES = 128
SUBLANES = 8
VMEM_LIMIT_BYTES = 56 * 1024 * 1024

TOKEN_TILE = 512
ATTN_Q_TILE = 512
ATTN_KV_TILE = 512
PAGES_PER_STEP = 8

_NT = (((1,), (1,)), ((), ()))


def _bf16(x):
    return x.astype(jnp.bfloat16)


def _layernorm(y, g, b):
    mu = jnp.mean(y, axis=-1, keepdims=True)
    yc = y - mu
    var = jnp.mean(yc * yc, axis=-1, keepdims=True)
    return yc * lax.rsqrt(var + LN_EPS) * g + b


def _const_spec(shape):
    nd = len(shape)
    return pl.BlockSpec(shape, lambda *_: (0,) * nd, pipeline_mode=pl.Buffered(1))


def _params(n_axes):
    return pltpu.CompilerParams(
        dimension_semantics=("arbitrary",) * n_axes,
        vmem_limit_bytes=VMEM_LIMIT_BYTES)


def _conv_prompt_body(x_ref, w_in_ref, taps_ref, w_out_ref, g_ref, b_ref,
                      o_ref, tail_ref, carry_ref, *, tiles_per_seq):
    i = pl.program_id(0)
    x = x_ref[...]
    t, d = x.shape
    h3 = jnp.dot(_bf16(x), w_in_ref[...], preferred_element_type=jnp.float32)
    gb, gc, h = h3[:, :d], h3[:, d:2 * d], h3[:, 2 * d:]
    u = gc * h

    @pl.when(i % tiles_per_seq == 0)
    def _():
        carry_ref[...] = jnp.zeros_like(carry_ref)

    row = lax.broadcasted_iota(jnp.int32, (t, 1), 0)
    c6 = carry_ref[SUBLANES - 2:SUBLANES - 1, :]
    c7 = carry_ref[SUBLANES - 1:SUBLANES, :]
    u1 = jnp.where(row == 0, c7, pltpu.roll(u, 1, 0))
    u2 = jnp.where(row == 0, c6, jnp.where(row == 1, c7, pltpu.roll(u, 2, 0)))
    taps = taps_ref[...]
    conv = taps[0:1, :] * u2 + taps[1:2, :] * u1 + taps[2:3, :] * u
    y = jnp.dot(_bf16(gb * conv), w_out_ref[...], preferred_element_type=jnp.float32)
    o_ref[...] = _layernorm(ALPHA * x + y, g_ref[...], b_ref[...])

    carry_ref[...] = u[t - SUBLANES:, :]

    @pl.when(i % tiles_per_seq == tiles_per_seq - 1)
    def _():
        tail_ref[...] = u[t - SUBLANES:, :]


def _conv_prompt(x, w_in, taps, w_out, g, b, seq_len):
    n, d = x.shape
    tm = TOKEN_TILE
    tiles_per_seq = seq_len // tm
    return pl.pallas_call(
        functools.partial(_conv_prompt_body, tiles_per_seq=tiles_per_seq),
        grid=(n // tm,),
        in_specs=[
            pl.BlockSpec((tm, d), lambda i: (i, 0)),
            _const_spec(w_in.shape), _const_spec(taps.shape), _const_spec(w_out.shape),
            _const_spec(g.shape), _const_spec(b.shape),
        ],
        out_specs=[
            pl.BlockSpec((tm, d), lambda i: (i, 0)),
            pl.BlockSpec((SUBLANES, d), lambda i: (i // tiles_per_seq, 0)),
        ],
        out_shape=[
            jax.ShapeDtypeStruct((n, d), jnp.float32),
            jax.ShapeDtypeStruct((n // seq_len * SUBLANES, d), jnp.float32),
        ],
        scratch_shapes=[pltpu.VMEM((SUBLANES, d), jnp.float32)],
        compiler_params=_params(1),
        name="conv_prompt",
    )(x, w_in, taps, w_out, g, b)


def _conv_sample_body(x_ref, st_ref, w_in_ref, taps_ref, w_out_ref, g_ref, b_ref,
                      o_ref, u_ref):
    x = x_ref[...]
    t, d = x.shape
    h3 = jnp.dot(_bf16(x), w_in_ref[...], preferred_element_type=jnp.float32)
    gb, gc, h = h3[:, :d], h3[:, d:2 * d], h3[:, 2 * d:]
    u = gc * h
    st = st_ref[...]
    pos = lax.broadcasted_iota(jnp.int32, (t, 1), 0) % SUBLANES
    u1 = jnp.where(pos == 0, pltpu.roll(st, t - 1, 0), pltpu.roll(u, 1, 0))
    u2 = jnp.where(pos < 2, st, pltpu.roll(u, 2, 0))
    taps = taps_ref[...]
    conv = taps[0:1, :] * u2 + taps[1:2, :] * u1 + taps[2:3, :] * u
    y = jnp.dot(_bf16(gb * conv), w_out_ref[...], preferred_element_type=jnp.float32)
    o_ref[...] = _layernorm(ALPHA * x + y, g_ref[...], b_ref[...])
    u_ref[...] = u


def _conv_sample(x, st, w_in, taps, w_out, g, b):
    n, d = x.shape
    return pl.pallas_call(
        _conv_sample_body,
        grid=(1,),
        in_specs=[_const_spec(a.shape) for a in (x, st, w_in, taps, w_out, g, b)],
        out_specs=[_const_spec((n, d)), _const_spec((n, d))],
        out_shape=[jax.ShapeDtypeStruct((n, d), jnp.float32)] * 2,
        compiler_params=_params(1),
        name="conv_sample",
    )(x, st, w_in, taps, w_out, g, b)


def _ffn_body(x_ref, w_in_ref, w_out_ref, g_ref, b_ref, o_ref):
    x = x_ref[...]
    f = w_out_ref.shape[0]
    h = jnp.dot(_bf16(x), w_in_ref[...], preferred_element_type=jnp.float32)
    gate, up = h[:, :f], h[:, f:]
    a = gate * jax.nn.sigmoid(gate) * up
    y = jnp.dot(_bf16(a), w_out_ref[...], preferred_element_type=jnp.float32)
    o_ref[...] = _layernorm(ALPHA * x + y, g_ref[...], b_ref[...])


def _ffn(x, w_in, w_out, g, b):
    n, d = x.shape
    tm = min(TOKEN_TILE, n)
    return pl.pallas_call(
        _ffn_body,
        grid=(n // tm,),
        in_specs=[
            pl.BlockSpec((tm, d), lambda i: (i, 0)),
            _const_spec(w_in.shape), _const_spec(w_out.shape),
            _const_spec(g.shape), _const_spec(b.shape),
        ],
        out_specs=pl.BlockSpec((tm, d), lambda i: (i, 0)),
        out_shape=jax.ShapeDtypeStruct((n, d), jnp.float32),
        compiler_params=_params(1),
        name="ffn",
    )(x, w_in, w_out, g, b)


def _rope_tables(pos):
    half = ROT_DIM // 2
    inv = jnp.power(ROPE_THETA, -jnp.arange(0, ROT_DIM, 2, dtype=jnp.float32) / ROT_DIM)
    ang = pos.astype(jnp.float32)[:, None] * inv[None, :]
    cos, sin = jnp.cos(ang), jnp.sin(ang)
    dd = jnp.arange(LANES) % HEAD_DIM
    cos_l = jnp.take(cos, dd % half, axis=1)
    sin_l = jnp.take(sin, dd % half, axis=1)
    c = jnp.where(dd[None, :] < ROT_DIM, cos_l, 1.0)
    s_up = jnp.where(dd[None, :] < half, -sin_l, 0.0)
    s_dn = jnp.where((dd[None, :] >= half) & (dd[None, :] < ROT_DIM), sin_l, 0.0)
    return c, s_up, s_dn


def _rope(x, c, s_up, s_dn):
    half = ROT_DIM // 2
    outs = []
    for g in range(x.shape[1] // LANES):
        xg = x[:, g * LANES:(g + 1) * LANES]
        x_up = pltpu.roll(xg, LANES - half, 1)
        x_dn = pltpu.roll(xg, half, 1)
        outs.append(xg * c + x_up * s_up + x_dn * s_dn)
    return jnp.concatenate(outs, axis=1)


def _qkv_rows(x_ref, w_ref, c_ref, su_ref, sd_ref):
    x = x_ref[...]
    d = x.shape[1]
    h3 = jnp.dot(_bf16(x), w_ref[...], preferred_element_type=jnp.float32)
    c, su, sd = c_ref[...], su_ref[...], sd_ref[...]
    q = _rope(h3[:, :d], c, su, sd)
    k = _rope(h3[:, d:2 * d], c, su, sd)
    return q, k, h3[:, 2 * d:]


def _qkv_sample_body(x_ref, w_ref, c_ref, su_ref, sd_ref, q_ref, k_ref, v_ref):
    q, k, v = _qkv_rows(x_ref, w_ref, c_ref, su_ref, sd_ref)
    q_ref[...] = q * SCALE
    k_ref[...] = k
    v_ref[...] = v


def _qkv_sample(x, w, tables):
    n, d = x.shape
    specs = [_const_spec(a.shape) for a in (x, w) + tuple(tables)]
    return pl.pallas_call(
        _qkv_sample_body,
        grid=(1,),
        in_specs=specs,
        out_specs=[_const_spec((n, d))] * 3,
        out_shape=[jax.ShapeDtypeStruct((n, d), jnp.float32)] * 3,
        compiler_params=_params(1),
        name="qkv_sample",
    )(x, w, *tables)


def _qkv_prompt_body(x_ref, w_ref, c_ref, su_ref, sd_ref,
                     qt_ref, kb_ref, kt_ref, vt_ref, v_ref):
    q, k, v = _qkv_rows(x_ref, w_ref, c_ref, su_ref, sd_ref)
    qt_ref[0] = _bf16((q * (SCALE * LOG2_E)).T)
    kb_ref[...] = _bf16(k)
    kt_ref[0] = k.T
    vt_ref[0] = _bf16(v.T)
    v_ref[...] = v


def _qkv_prompt(x, w, tables, seq_len):
    n, d = x.shape
    tm = TOKEN_TILE
    tps = seq_len // tm
    tspec = pl.BlockSpec((tm, LANES), lambda i: (i % tps, 0))
    row = pl.BlockSpec((tm, d), lambda i: (i, 0))
    blk = pl.BlockSpec((1, d, tm), lambda i: (i, 0, 0))
    return pl.pallas_call(
        _qkv_prompt_body,
        grid=(n // tm,),
        in_specs=[row, _const_spec(w.shape), tspec, tspec, tspec],
        out_specs=[blk, row, pl.BlockSpec((1, d, tm), lambda i: (i // tps, 0, i % tps)), blk, row],
        out_shape=[
            jax.ShapeDtypeStruct((n // tm, d, tm), jnp.bfloat16),
            jax.ShapeDtypeStruct((n, d), jnp.bfloat16),
            jax.ShapeDtypeStruct((n // seq_len, d, seq_len), jnp.float32),
            jax.ShapeDtypeStruct((n // tm, d, tm), jnp.bfloat16),
            jax.ShapeDtypeStruct((n, d), jnp.float32),
        ],
        compiler_params=_params(1),
        name="qkv_prompt",
    )(x, w, *tables)


def _lambda(lamv_ref):
    lv = lamv_ref[...]
    d1 = jnp.sum(lv[0:1, :] * lv[1:2, :], axis=-1, keepdims=True)
    d2 = jnp.sum(lv[2:3, :] * lv[3:4, :], axis=-1, keepdims=True)
    return jnp.exp(d1) - jnp.exp(d2) + LAM_INIT


def _subln(o, g):
    o = o * lax.rsqrt(jnp.mean(o * o, axis=-1, keepdims=True) + SUBLN_EPS)
    return o * g * (1.0 - LAM_INIT)


def _prompt_attn_body(qt_ref, k_ref, vt_ref, lamv_ref, gcol_ref, o_ref, acc_ref):
    tq, tk = ATTN_Q_TILE, ATTN_KV_TILE
    i = pl.program_id(2)
    neg = jnp.finfo(jnp.float32).min
    qt = qt_ref[0]
    feat = lax.broadcasted_iota(jnp.int32, (2 * HEAD_DIM, 1), 0)
    qt_sub = (jnp.where(feat < HEAD_DIM, qt, 0), jnp.where(feat >= HEAD_DIM, qt, 0))
    acc_ref[...] = jnp.zeros_like(acc_ref)

    def update(kv, stats, masked):
        kblk = k_ref[0, pl.ds(pl.multiple_of(kv * tk, tk), tk), :]
        vblk = vt_ref[0, kv]
        new_stats = []
        for c in range(2):
            m_old, l_old = stats[c]
            st = jnp.dot(kblk, qt_sub[c], preferred_element_type=jnp.float32)
            if masked:
                key = lax.broadcasted_iota(jnp.int32, (tk, tq), 0)
                qry = lax.broadcasted_iota(jnp.int32, (tk, tq), 1)
                st = jnp.where(key <= qry, st, neg)
            m_new = jnp.maximum(m_old, jnp.max(st, axis=0, keepdims=True))
            alpha = jnp.exp2(m_old - m_new)
            p = jnp.exp2(st - m_new)
            l_new = alpha * l_old + jnp.sum(p, axis=0, keepdims=True)
            acc_ref[c] = alpha * acc_ref[c] + jnp.dot(
                vblk, _bf16(p), preferred_element_type=jnp.float32)
            new_stats.append((m_new, l_new))
        return tuple(new_stats)

    init = ((jnp.full((1, tq), neg, jnp.float32), jnp.zeros((1, tq), jnp.float32)),) * 2
    stats = lax.fori_loop(0, i, lambda kv, st: update(kv, st, masked=False), init)
    stats = update(i, stats, masked=True)

    lam = _lambda(lamv_ref)
    o = acc_ref[0] / stats[0][1] - lam * (acc_ref[1] / stats[1][1])
    o = o * lax.rsqrt(jnp.mean(o * o, axis=0, keepdims=True) + SUBLN_EPS)
    o = o * gcol_ref[...] * (1.0 - LAM_INIT)
    o_ref[0] = o.T.astype(o_ref.dtype)


def _prompt_attn(qt, k, vt, lamv, gcol):
    b, s, d = k.shape
    tq, tk = ATTN_Q_TILE, ATTN_KV_TILE
    nq = s // tq
    assert tq == tk and s % tq == 0
    return pl.pallas_call(
        _prompt_attn_body,
        grid=(b, N_HEADS, nq),
        in_specs=[
            pl.BlockSpec((1, V_DIM, tq), lambda bi, j, i: (bi * nq + i, j, 0)),
            pl.BlockSpec((1, s, V_DIM), lambda bi, j, i: (bi, 0, j)),
            pl.BlockSpec((1, s // tk, V_DIM, tk), lambda bi, j, i: (bi, 0, j, 0)),
            _const_spec(lamv.shape), _const_spec(gcol.shape),
        ],
        out_specs=pl.BlockSpec((1, tq, V_DIM), lambda bi, j, i: (bi, i, j)),
        out_shape=jax.ShapeDtypeStruct((b, s, d), jnp.bfloat16),
        scratch_shapes=[pltpu.VMEM((2, V_DIM, tq), jnp.float32)],
        compiler_params=_params(3),
        name="prompt_attn",
    )(qt, k, vt, lamv, gcol)


def _sample_attn_body(pt_ref, q_ref, kn_ref, vn_ref, lamv_ref, g_ref, *refs,
                      n_steps):
    pps = PAGES_PER_STEP
    k_refs, v_refs = refs[:pps], refs[pps:2 * pps]
    o_ref = refs[2 * pps]
    m_ref, l_ref, acc_ref = refs[2 * pps + 1:]
    c = pl.program_id(1)
    ds = q_ref.shape[2]
    page = v_refs[0].shape[1]
    neg = jnp.finfo(jnp.float32).min
    q3 = _bf16(q_ref[0])

    def pair_rows(s):
        return s.reshape(N_HEADS, 2 * ds, s.shape[2])

    def scores(k3):
        return pair_rows(jnp.einsum("hqd,htd->hqt", q3, k3, preferred_element_type=jnp.float32))

    def scores_t(kt3):
        return pair_rows(jnp.einsum("hqd,hdt->hqt", q3, kt3, preferred_element_type=jnp.float32))

    def weighted(p, v3):
        return jnp.einsum("jrt,jte->jre", _bf16(p), v3, preferred_element_type=jnp.float32)

    @pl.when(c == 0)
    def _():
        kn = kn_ref[0]
        vn = vn_ref[0]
        kn3 = _bf16(jnp.concatenate(
            [kn, jnp.zeros((kn.shape[0], page - ds, kn.shape[2]), jnp.float32)], axis=1))
        vn3 = _bf16(jnp.concatenate(
            [vn, jnp.zeros((vn.shape[0], page - ds, vn.shape[2]), jnp.float32)], axis=1))
        s = scores(kn3)
        qi = lax.broadcasted_iota(jnp.int32, s.shape, 1) % ds
        tt = lax.broadcasted_iota(jnp.int32, s.shape, 2)
        s = jnp.where(tt <= qi, s, neg)
        m = jnp.max(s, axis=-1, keepdims=True)
        p = jnp.exp(s - m)
        m_ref[...] = m
        l_ref[...] = jnp.sum(p, axis=-1, keepdims=True)
        acc_ref[...] = weighted(p, vn3)

    kt3 = jnp.concatenate([_bf16(k_refs[pg][0]) for pg in range(pps)], axis=2)
    v3 = jnp.concatenate(
        [_bf16(pltpu.einshape("tje->jte", v_refs[pg][0])) for pg in range(pps)], axis=1)
    s = scores_t(kt3)
    m_old = m_ref[...]
    m_new = jnp.maximum(m_old, jnp.max(s, axis=-1, keepdims=True))
    alpha = jnp.exp(m_old - m_new)
    p = jnp.exp(s - m_new)
    l_ref[...] = alpha * l_ref[...] + jnp.sum(p, axis=-1, keepdims=True)
    acc_ref[...] = alpha * acc_ref[...] + weighted(p, v3)
    m_ref[...] = m_new

    @pl.when(c == n_steps - 1)
    def _():
        lam = _lambda(lamv_ref)
        g = g_ref[...]
        o = acc_ref[...] / l_ref[...]
        o = o[:, :ds, :] - lam * o[:, ds:, :]
        o = _subln(o, g)
        for j in range(N_HEADS):
            o_ref[0, :, j * V_DIM:(j + 1) * V_DIM] = o[j]


def _sample_attn(q3, k_new3, v_new3, cache_kt, cache_v, page_table, lamv, g):
    db, nh2, ds, hd = q3.shape
    n_pages = page_table.shape[1]
    page = cache_v.shape[1]
    cache_k = cache_kt
    pps = PAGES_PER_STEP
    n_steps = n_pages // pps
    d = nh2 * hd
    assert n_pages % pps == 0 and ds == SUBLANES

    def page_spec(shape, pg):
        return pl.BlockSpec(
            (1,) + shape[1:],
            lambda b, c, pt: (pt[b * n_pages + c * pps + pg], 0, 0, 0))

    def seq_spec(shape):
        return pl.BlockSpec((1,) + shape[1:], lambda b, c, pt: (b,) + (0,) * (len(shape) - 1))

    grid_spec = pltpu.PrefetchScalarGridSpec(
        num_scalar_prefetch=1,
        grid=(db, n_steps),
        in_specs=[seq_spec(q3.shape), seq_spec(k_new3.shape), seq_spec(v_new3.shape),
                  pl.BlockSpec(lamv.shape, lambda b, c, pt: (0, 0)),
                  pl.BlockSpec(g.shape, lambda b, c, pt: (0, 0))]
                 + [page_spec(cache_k.shape, pg) for pg in range(pps)]
                 + [page_spec(cache_v.shape, pg) for pg in range(pps)],
        out_specs=seq_spec((db, ds, d)),
        scratch_shapes=[
            pltpu.VMEM((N_HEADS, 2 * ds, 1), jnp.float32),
            pltpu.VMEM((N_HEADS, 2 * ds, 1), jnp.float32),
            pltpu.VMEM((N_HEADS, 2 * ds, V_DIM), jnp.float32),
        ],
    )
    return pl.pallas_call(
        functools.partial(_sample_attn_body, n_steps=n_steps),
        grid_spec=grid_spec,
        out_shape=jax.ShapeDtypeStruct((db, ds, d), jnp.float32),
        compiler_params=_params(2),
        name="sample_attn",
    )(page_table.reshape(-1), q3, k_new3, v_new3, lamv, g,
      *([cache_k] * pps), *([cache_v] * pps))


def _proj_ln_body(o_ref, x_ref, w_ref, g_ref, b_ref, y_ref):
    y = jnp.dot(_bf16(o_ref[...]), w_ref[...], preferred_element_type=jnp.float32)
    y_ref[...] = _layernorm(ALPHA * x_ref[...] + y, g_ref[...], b_ref[...])


def _proj_ln(o, x, w, g, b):
    n, d = x.shape
    tm = min(TOKEN_TILE, n)
    row = pl.BlockSpec((tm, d), lambda i: (i, 0))
    return pl.pallas_call(
        _proj_ln_body,
        grid=(n // tm,),
        in_specs=[row, row, _const_spec(w.shape), _const_spec(g.shape), _const_spec(b.shape)],
        out_specs=row,
        out_shape=jax.ShapeDtypeStruct((n, d), jnp.float32),
        compiler_params=_params(1),
        name="proj_ln",
    )(o, x, w, g, b)


def kernel(x_prompt, x_sample, state_conv, cache_k, cache_v, page_table, w_conv_in, w_conv, w_conv_out, w_qkv, lambda_q1, lambda_k1, lambda_q2, lambda_k2, subln_g, w_attn_out, ln_mix_g, ln_mix_b, w_ffn_in, w_ffn_out, ln_ffn_g, ln_ffn_b):
    b, s, d = x_prompt.shape
    db, ds, _ = x_sample.shape
    n_pool, page = cache_k.shape[:2]
    past_len = page_table.shape[1] * page
    f32 = jnp.float32
    assert ds == SUBLANES and s % TOKEN_TILE == 0

    w_conv_in_b, w_conv_out_b = _bf16(w_conv_in), _bf16(w_conv_out)
    w_qkv_b, w_attn_out_b = _bf16(w_qkv), _bf16(w_attn_out)
    w_ffn_in_b, w_ffn_out_b = _bf16(w_ffn_in), _bf16(w_ffn_out)
    taps = w_conv.astype(f32)
    row = lambda a: a.reshape(1, -1).astype(f32)
    lamv = jnp.stack([lambda_q1, lambda_k1, lambda_q2, lambda_k2]).astype(f32)
    g_sub = row(subln_g)

    xp = x_prompt.reshape(b * s, d)
    xs = x_sample.reshape(db * ds, d)

    i = 0
    xp, tail_p = _conv_prompt(xp, w_conv_in_b, taps, w_conv_out_b,
                              row(ln_mix_g[i]), row(ln_mix_b[i]), s)
    conv_p = tail_p.reshape(b, SUBLANES, d)[:, SUBLANES - (CONV_WIDTH - 1):]
    st = jnp.pad(state_conv, ((0, 0), (0, ds - (CONV_WIDTH - 1)), (0, 0))).reshape(db * ds, d)
    xs, u_s = _conv_sample(xs, st, w_conv_in_b, taps, w_conv_out_b,
                           row(ln_mix_g[i]), row(ln_mix_b[i]))
    conv_s = u_s.reshape(db, ds, d)[:, ds - (CONV_WIDTH - 1):]
    xp = _ffn(xp, w_ffn_in_b[i], w_ffn_out_b[i], row(ln_ffn_g[i]), row(ln_ffn_b[i]))
    xs = _ffn(xs, w_ffn_in_b[i], w_ffn_out_b[i], row(ln_ffn_g[i]), row(ln_ffn_b[i]))

    i = 1
    tab_p = _rope_tables(jnp.arange(s))
    tab_s = _rope_tables(past_len + jnp.arange(db * ds) % ds)
    qt_p, kb_p, kt_p, vt_p, v_p = _qkv_prompt(xp, w_qkv_b, tab_p, s)
    qs, k_s, v_s = _qkv_sample(xs, w_qkv_b, tab_s)
    k_p = kt_p.reshape(b, 2 * N_HEADS, HEAD_DIM, s).transpose(0, 3, 1, 2)

    op = _prompt_attn(qt_p, kb_p.reshape(b, s, d),
                      vt_p.reshape(b, s // ATTN_KV_TILE, d, ATTN_KV_TILE),
                      lamv, g_sub.reshape(-1, 1))
    heads_first = lambda a, nh: a.reshape(db, ds, nh, d // nh).transpose(0, 2, 1, 3)
    os_ = _sample_attn(heads_first(qs, 2 * N_HEADS), heads_first(k_s, 2 * N_HEADS),
                       heads_first(v_s, N_HEADS), cache_k.transpose(0, 2, 3, 1), cache_v,
                       page_table, lamv, g_sub)
    xp = _proj_ln(op.reshape(b * s, d), xp, w_attn_out_b, row(ln_mix_g[i]), row(ln_mix_b[i]))
    xs = _proj_ln(os_.reshape(db * ds, d), xs, w_attn_out_b, row(ln_mix_g[i]), row(ln_mix_b[i]))
    xp = _ffn(xp, w_ffn_in_b[i], w_ffn_out_b[i], row(ln_ffn_g[i]), row(ln_ffn_b[i]))
    xs = _ffn(xs, w_ffn_in_b[i], w_ffn_out_b[i], row(ln_ffn_g[i]), row(ln_ffn_b[i]))

    return (xp.reshape(b, s, d), xs.reshape(db, ds, d), conv_p,
            k_p, v_p.reshape(b, s, N_HEADS, V_DIM),
            conv_s,
            k_s.reshape(db, ds, 2 * N_HEADS, HEAD_DIM), v_s.reshape(db, ds, N_HEADS, V_DIM))
```

```python
import functools
import math

import jax
import jax.numpy as jnp
from jax import lax
from jax.experimental import pallas as pl
from jax.experimental.pallas import tpu as pltpu

N_HEADS = 8
HEAD_DIM = 64
V_DIM = 2 * HEAD_DIM
ROT_DIM = HEAD_DIM // 4
ROPE_THETA = 500000.0
CONV_WIDTH = 3
DEPTH = 2
LN_EPS = 1e-5
SUBLN_EPS = 1e-5
ALPHA = (2 * DEPTH) ** 0.25
SCALE = HEAD_DIM ** -0.5
LOG2_E = math.log2(math.e)
ATTN_LAYER = 1
LAM_INIT = 0.8 - 0.6 * math.exp(-0.3 * ATTN_LAYER)

LANES = 128
SUBLANES = 8
BF16_SUBLANES = 16
VMEM_LIMIT_BYTES = 56 * 1024 * 1024

TOKEN_TILE = 512
ATTN_Q_TILE = 512
ATTN_KV_TILE = 512
PAGES_PER_STEP = 8

_NT = (((1,), (1,)), ((), ()))


def _bf16(x):
    return x.astype(jnp.bfloat16)


def _layernorm(y, g, b):
    mu = jnp.mean(y, axis=-1, keepdims=True)
    yc = y - mu
    var = jnp.mean(yc * yc, axis=-1, keepdims=True)
    return yc * lax.rsqrt(var + LN_EPS) * g + b


def _const_spec(shape):
    nd = len(shape)
    return pl.BlockSpec(shape, lambda *_: (0,) * nd, pipeline_mode=pl.Buffered(1))


def _params(n_axes):
    return pltpu.CompilerParams(
        dimension_semantics=("arbitrary",) * n_axes,
        vmem_limit_bytes=VMEM_LIMIT_BYTES)


def _conv_prompt_body(x_ref, w_in_ref, taps_ref, w_out_ref, g_ref, b_ref,
                      o_ref, tail_ref, carry_ref, *, tiles_per_seq):
    i = pl.program_id(0)
    x = x_ref[...]
    t, d = x.shape
    h3 = jnp.dot(_bf16(x), w_in_ref[...], preferred_element_type=jnp.float32)
    gb, gc, h = h3[:, :d], h3[:, d:2 * d], h3[:, 2 * d:]
    u = gc * h

    @pl.when(i % tiles_per_seq == 0)
    def _():
        carry_ref[...] = jnp.zeros_like(carry_ref)

    row = lax.broadcasted_iota(jnp.int32, (t, 1), 0)
    c6 = carry_ref[SUBLANES - 2:SUBLANES - 1, :]
    c7 = carry_ref[SUBLANES - 1:SUBLANES, :]
    u1 = jnp.where(row == 0, c7, pltpu.roll(u, 1, 0))
    u2 = jnp.where(row == 0, c6, jnp.where(row == 1, c7, pltpu.roll(u, 2, 0)))
    taps = taps_ref[...]
    conv = taps[0:1, :] * u2 + taps[1:2, :] * u1 + taps[2:3, :] * u
    y = jnp.dot(_bf16(gb * conv), w_out_ref[...], preferred_element_type=jnp.float32)
    o_ref[...] = _layernorm(ALPHA * x + y, g_ref[...], b_ref[...])

    carry_ref[...] = u[t - SUBLANES:, :]

    @pl.when(i % tiles_per_seq == tiles_per_seq - 1)
    def _():
        tail_ref[...] = u[t - SUBLANES:, :]


def _conv_prompt(x, w_in, taps, w_out, g, b, seq_len):
    n, d = x.shape
    tm = TOKEN_TILE
    tiles_per_seq = seq_len // tm
    return pl.pallas_call(
        functools.partial(_conv_prompt_body, tiles_per_seq=tiles_per_seq),
        grid=(n // tm,),
        in_specs=[
            pl.BlockSpec((tm, d), lambda i: (i, 0)),
            _const_spec(w_in.shape), _const_spec(taps.shape), _const_spec(w_out.shape),
            _const_spec(g.shape), _const_spec(b.shape),
        ],
        out_specs=[
            pl.BlockSpec((tm, d), lambda i: (i, 0)),
            pl.BlockSpec((SUBLANES, d), lambda i: (i // tiles_per_seq, 0)),
        ],
        out_shape=[
            jax.ShapeDtypeStruct((n, d), jnp.float32),
            jax.ShapeDtypeStruct((n // seq_len * SUBLANES, d), jnp.float32),
        ],
        scratch_shapes=[pltpu.VMEM((SUBLANES, d), jnp.float32)],
        compiler_params=_params(1),
        name="conv_prompt",
    )(x, w_in, taps, w_out, g, b)


def _conv_sample_body(x_ref, st_ref, w_in_ref, taps_ref, w_out_ref, g_ref, b_ref,
                      o_ref, u_ref):
    x = x_ref[...]
    t, d = x.shape
    h3 = jnp.dot(_bf16(x), w_in_ref[...], preferred_element_type=jnp.float32)
    gb, gc, h = h3[:, :d], h3[:, d:2 * d], h3[:, 2 * d:]
    u = gc * h
    st = st_ref[...]
    pos = lax.broadcasted_iota(jnp.int32, (t, 1), 0) % SUBLANES
    u1 = jnp.where(pos == 0, pltpu.roll(st, t - 1, 0), pltpu.roll(u, 1, 0))
    u2 = jnp.where(pos < 2, st, pltpu.roll(u, 2, 0))
    taps = taps_ref[...]
    conv = taps[0:1, :] * u2 + taps[1:2, :] * u1 + taps[2:3, :] * u
    y = jnp.dot(_bf16(gb * conv), w_out_ref[...], preferred_element_type=jnp.float32)
    o_ref[...] = _layernorm(ALPHA * x + y, g_ref[...], b_ref[...])
    u_ref[...] = u


def _conv_sample(x, st, w_in, taps, w_out, g, b):
    n, d = x.shape
    return pl.pallas_call(
        _conv_sample_body,
        grid=(1,),
        in_specs=[_const_spec(a.shape) for a in (x, st, w_in, taps, w_out, g, b)],
        out_specs=[_const_spec((n, d)), _const_spec((n, d))],
        out_shape=[jax.ShapeDtypeStruct((n, d), jnp.float32)] * 2,
        compiler_params=_params(1),
        name="conv_sample",
    )(x, st, w_in, taps, w_out, g, b)


def _ffn_body(x_ref, w_in_ref, w_out_ref, g_ref, b_ref, o_ref):
    x = x_ref[...]
    f = w_out_ref.shape[0]
    h = jnp.dot(_bf16(x), w_in_ref[...], preferred_element_type=jnp.float32)
    gate, up = h[:, :f], h[:, f:]
    a = gate * jax.nn.sigmoid(gate) * up
    y = jnp.dot(_bf16(a), w_out_ref[...], preferred_element_type=jnp.float32)
    o_ref[...] = _layernorm(ALPHA * x + y, g_ref[...], b_ref[...])


def _ffn(x, w_in, w_out, g, b):
    n, d = x.shape
    tm = min(TOKEN_TILE, n)
    return pl.pallas_call(
        _ffn_body,
        grid=(n // tm,),
        in_specs=[
            pl.BlockSpec((tm, d), lambda i: (i, 0)),
            _const_spec(w_in.shape), _const_spec(w_out.shape),
            _const_spec(g.shape), _const_spec(b.shape),
        ],
        out_specs=pl.BlockSpec((tm, d), lambda i: (i, 0)),
        out_shape=jax.ShapeDtypeStruct((n, d), jnp.float32),
        compiler_params=_params(1),
        name="ffn",
    )(x, w_in, w_out, g, b)


def _rope_tables(pos):
    half = ROT_DIM // 2
    inv = jnp.power(ROPE_THETA, -jnp.arange(0, ROT_DIM, 2, dtype=jnp.float32) / ROT_DIM)
    ang = pos.astype(jnp.float32)[:, None] * inv[None, :]
    cos, sin = jnp.cos(ang), jnp.sin(ang)
    dd = jnp.arange(LANES) % HEAD_DIM
    cos_l = jnp.take(cos, dd % half, axis=1)
    sin_l = jnp.take(sin, dd % half, axis=1)
    c = jnp.where(dd[None, :] < ROT_DIM, cos_l, 1.0)
    s_up = jnp.where(dd[None, :] < half, -sin_l, 0.0)
    s_dn = jnp.where((dd[None, :] >= half) & (dd[None, :] < ROT_DIM), sin_l, 0.0)
    return c, s_up, s_dn


def _rope(x, c, s_up, s_dn):
    half = ROT_DIM // 2
    outs = []
    for g in range(x.shape[1] // LANES):
        xg = x[:, g * LANES:(g + 1) * LANES]
        x_up = pltpu.roll(xg, LANES - half, 1)
        x_dn = pltpu.roll(xg, half, 1)
        outs.append(xg * c + x_up * s_up + x_dn * s_dn)
    return jnp.concatenate(outs, axis=1)


def _qkv_rows(x_ref, w_ref, c_ref, su_ref, sd_ref):
    x = x_ref[...]
    d = x.shape[1]
    h3 = jnp.dot(_bf16(x), w_ref[...], preferred_element_type=jnp.float32)
    c, su, sd = c_ref[...], su_ref[...], sd_ref[...]
    q = _rope(h3[:, :d], c, su, sd)
    k = _rope(h3[:, d:2 * d], c, su, sd)
    return q, k, h3[:, 2 * d:]


def _qkv_sample_body(x_ref, w_ref, c_ref, su_ref, sd_ref, q_ref, k_ref, v_ref):
    q, k, v = _qkv_rows(x_ref, w_ref, c_ref, su_ref, sd_ref)
    q_ref[...] = q * SCALE
    k_ref[...] = k
    v_ref[...] = v


def _qkv_sample(x, w, tables):
    n, d = x.shape
    specs = [_const_spec(a.shape) for a in (x, w) + tuple(tables)]
    return pl.pallas_call(
        _qkv_sample_body,
        grid=(1,),
        in_specs=specs,
        out_specs=[_const_spec((n, d))] * 3,
        out_shape=[jax.ShapeDtypeStruct((n, d), jnp.float32)] * 3,
        compiler_params=_params(1),
        name="qkv_sample",
    )(x, w, *tables)


def _qkv_prompt_body(x_ref, w_ref, c_ref, su_ref, sd_ref,
                     qt_ref, kb_ref, kt_ref, vt_ref, v_ref):
    q, k, v = _qkv_rows(x_ref, w_ref, c_ref, su_ref, sd_ref)
    qt_ref[0] = _bf16((q * (SCALE * LOG2_E)).T)
    kb_ref[...] = _bf16(k)
    kt_ref[0] = k.T
    vt_ref[0] = _bf16(v.T)
    v_ref[...] = v


def _qkv_prompt(x, w, tables, seq_len):
    n, d = x.shape
    tm = TOKEN_TILE
    tps = seq_len // tm
    tspec = pl.BlockSpec((tm, LANES), lambda i: (i % tps, 0))
    row = pl.BlockSpec((tm, d), lambda i: (i, 0))
    blk = pl.BlockSpec((1, d, tm), lambda i: (i, 0, 0))
    return pl.pallas_call(
        _qkv_prompt_body,
        grid=(n // tm,),
        in_specs=[row, _const_spec(w.shape), tspec, tspec, tspec],
        out_specs=[blk, row, pl.BlockSpec((1, d, tm), lambda i: (i // tps, 0, i % tps)), blk, row],
        out_shape=[
            jax.ShapeDtypeStruct((n // tm, d, tm), jnp.bfloat16),
            jax.ShapeDtypeStruct((n, d), jnp.bfloat16),
            jax.ShapeDtypeStruct((n // seq_len, d, seq_len), jnp.float32),
            jax.ShapeDtypeStruct((n // tm, d, tm), jnp.bfloat16),
            jax.ShapeDtypeStruct((n, d), jnp.float32),
        ],
        compiler_params=_params(1),
        name="qkv_prompt",
    )(x, w, *tables)


def _lambda(lamv_ref):
    lv = lamv_ref[...]
    d1 = jnp.sum(lv[0:1, :] * lv[1:2, :], axis=-1, keepdims=True)
    d2 = jnp.sum(lv[2:3, :] * lv[3:4, :], axis=-1, keepdims=True)
    return jnp.exp(d1) - jnp.exp(d2) + LAM_INIT


def _subln(o, g):
    o = o * lax.rsqrt(jnp.mean(o * o, axis=-1, keepdims=True) + SUBLN_EPS)
    return o * g * (1.0 - LAM_INIT)


def _prompt_attn_body(qlo_ref, qhi_ref, k_ref, vt_ref, lamv_ref, gcol_ref, o_ref,
                      q_ref, s0_ref, s1_ref, m_ref, acc_ref, *, n_tiles):
    tq, tk = ATTN_Q_TILE, ATTN_KV_TILE
    t_lo = pl.program_id(2)
    t_hi = n_tiles - 1 - t_lo
    neg = jnp.finfo(jnp.float32).min
    feat = lax.broadcasted_iota(jnp.int32, (2 * HEAD_DIM, 1), 0)
    for tile, ref in enumerate((qlo_ref, qhi_ref)):
        q_ref[tile, 0] = jnp.where(feat < HEAD_DIM, ref[0], 0)
        q_ref[tile, 1] = jnp.where(feat >= HEAD_DIM, ref[0], 0)
    ones = jnp.ones((acc_ref.shape[2] - V_DIM, tk), jnp.bfloat16)
    acc_ref[...] = jnp.zeros_like(acc_ref)
    m_ref[...] = jnp.full_like(m_ref, neg)
    s_refs = (s0_ref, s1_ref)

    def block(u):
        if u == 0:
            return 0, t_lo, True
        if u == 1:
            return 1, t_hi, True
        n = u - 2
        return jnp.where(n < t_lo, 0, 1), jnp.where(n < t_lo, n, n - t_lo), False

    def scores(u):
        tile, kv, masked = block(u)
        kblk = k_ref[0, pl.ds(pl.multiple_of(kv * tk, tk), tk), :]
        for c in range(2):
            st = jnp.dot(kblk, q_ref[tile, c], preferred_element_type=jnp.float32)
            if masked:
                key = lax.broadcasted_iota(jnp.int32, (tk, tq), 0)
                qry = lax.broadcasted_iota(jnp.int32, (tk, tq), 1)
                st = jnp.where(key <= qry, st, neg)
            s_refs[u % 2][c] = st

    def accumulate(u):
        tile, kv, _ = block(u)
        s_ref = s_refs[u % 2]
        vext = jnp.concatenate([vt_ref[0, kv], ones], axis=0)
        for c in range(2):
            m_old = m_ref[tile, c]
            m_new = jnp.maximum(m_old, jnp.max(s_ref[c], axis=0, keepdims=True))
            alpha = jnp.exp2(m_old - m_new)
            p = _bf16(jnp.exp2(s_ref[c] - m_new))
            acc_ref[tile, c] = alpha * acc_ref[tile, c] + jnp.dot(
                vext, p, preferred_element_type=jnp.float32)
            m_ref[tile, c] = m_new

    n_blocks = n_tiles + 1
    scores(0)
    for u in range(n_blocks):
        if u + 1 < n_blocks:
            scores(u + 1)
        accumulate(u)

    lam = _lambda(lamv_ref)
    for tile in range(2):
        o = (acc_ref[tile, 0, :V_DIM] / acc_ref[tile, 0, V_DIM:V_DIM + 1]
             - lam * (acc_ref[tile, 1, :V_DIM] / acc_ref[tile, 1, V_DIM:V_DIM + 1]))
        o = o * lax.rsqrt(jnp.mean(o * o, axis=0, keepdims=True) + SUBLN_EPS)
        o = o * gcol_ref[...] * (1.0 - LAM_INIT)
        o_ref[0, tile, 0] = o.T.astype(o_ref.dtype)


def _prompt_attn(qt, k, vt, lamv, gcol):
    b, s, d = k.shape
    tq, tk = ATTN_Q_TILE, ATTN_KV_TILE
    nq = s // tq
    assert tq == tk and s % tq == 0 and nq % 2 == 0
    return pl.pallas_call(
        functools.partial(_prompt_attn_body, n_tiles=nq),
        grid=(b, N_HEADS, nq // 2),
        in_specs=[
            pl.BlockSpec((1, V_DIM, tq), lambda bi, j, t: (bi * nq + t, j, 0)),
            pl.BlockSpec((1, V_DIM, tq), lambda bi, j, t: (bi * nq + nq - 1 - t, j, 0)),
            pl.BlockSpec((1, s, V_DIM), lambda bi, j, t: (bi, 0, j)),
            pl.BlockSpec((1, s // tk, V_DIM, tk), lambda bi, j, t: (bi, 0, j, 0)),
            _const_spec(lamv.shape), _const_spec(gcol.shape),
        ],
        out_specs=pl.BlockSpec((1, 2, 1, tq, V_DIM), lambda bi, j, t: (bi, 0, t, 0, j)),
        out_shape=jax.ShapeDtypeStruct((b, 2, nq // 2, tq, d), jnp.bfloat16),
        scratch_shapes=[
            pltpu.VMEM((2, 2, V_DIM, tq), jnp.bfloat16),
            pltpu.VMEM((2, tk, tq), jnp.float32),
            pltpu.VMEM((2, tk, tq), jnp.float32),
            pltpu.VMEM((2, 2, 1, tq), jnp.float32),
            pltpu.VMEM((2, 2, V_DIM + BF16_SUBLANES, tq), jnp.float32),
        ],
        compiler_params=_params(3),
        name="prompt_attn",
    )(qt, qt, k, vt, lamv, gcol)


def _paired_tile_slot(t, nq):
    return jnp.where(t < nq // 2, t, nq // 2 + nq - 1 - t)


def _sample_attn_body(pt_ref, q_ref, kn_ref, vn_ref, lamv_ref, g_ref, *refs,
                      n_steps):
    pps = PAGES_PER_STEP
    k_refs, v_refs = refs[:pps], refs[pps:2 * pps]
    o_ref = refs[2 * pps]
    m_ref, l_ref, acc_ref = refs[2 * pps + 1:]
    c = pl.program_id(1)
    ds = q_ref.shape[2]
    page = v_refs[0].shape[1]
    neg = jnp.finfo(jnp.float32).min
    q3 = _bf16(q_ref[0])

    def pair_rows(s):
        return s.reshape(N_HEADS, 2 * ds, s.shape[2])

    def scores(k3):
        return pair_rows(jnp.einsum("hqd,htd->hqt", q3, k3, preferred_element_type=jnp.float32))

    def scores_t(kt3):
        return pair_rows(jnp.einsum("hqd,hdt->hqt", q3, kt3, preferred_element_type=jnp.float32))

    def weighted(p, v3):
        return jnp.einsum("jrt,jte->jre", _bf16(p), v3, preferred_element_type=jnp.float32)

    @pl.when(c == 0)
    def _():
        kn = kn_ref[0]
        vn = vn_ref[0]
        kn3 = _bf16(jnp.concatenate(
            [kn, jnp.zeros((kn.shape[0], page - ds, kn.shape[2]), jnp.float32)], axis=1))
        vn3 = _bf16(jnp.concatenate(
            [vn, jnp.zeros((vn.shape[0], page - ds, vn.shape[2]), jnp.float32)], axis=1))
        s = scores(kn3)
        qi = lax.broadcasted_iota(jnp.int32, s.shape, 1) % ds
        tt = lax.broadcasted_iota(jnp.int32, s.shape, 2)
        s = jnp.where(tt <= qi, s, neg)
        m = jnp.max(s, axis=-1, keepdims=True)
        p = jnp.exp(s - m)
        m_ref[...] = m
        l_ref[...] = jnp.sum(p, axis=-1, keepdims=True)
        acc_ref[...] = weighted(p, vn3)

    kt3 = jnp.concatenate([_bf16(k_refs[pg][0]) for pg in range(pps)], axis=2)
    v3 = jnp.concatenate(
        [_bf16(pltpu.einshape("tje->jte", v_refs[pg][0])) for pg in range(pps)], axis=1)
    s = scores_t(kt3)
    m_old = m_ref[...]
    m_new = jnp.maximum(m_old, jnp.max(s, axis=-1, keepdims=True))
    alpha = jnp.exp(m_old - m_new)
    p = jnp.exp(s - m_new)
    l_ref[...] = alpha * l_ref[...] + jnp.sum(p, axis=-1, keepdims=True)
    acc_ref[...] = alpha * acc_ref[...] + weighted(p, v3)
    m_ref[...] = m_new

    @pl.when(c == n_steps - 1)
    def _():
        lam = _lambda(lamv_ref)
        g = g_ref[...]
        o = acc_ref[...] / l_ref[...]
        o = o[:, :ds, :] - lam * o[:, ds:, :]
        o = _subln(o, g)
        for j in range(N_HEADS):
            o_ref[0, :, j * V_DIM:(j + 1) * V_DIM] = o[j]


def _sample_attn(q3, k_new3, v_new3, cache_kt, cache_v, page_table, lamv, g):
    db, nh2, ds, hd = q3.shape
    n_pages = page_table.shape[1]
    page = cache_v.shape[1]
    cache_k = cache_kt
    pps = PAGES_PER_STEP
    n_steps = n_pages // pps
    d = nh2 * hd
    assert n_pages % pps == 0 and ds == SUBLANES

    def page_spec(shape, pg):
        return pl.BlockSpec(
            (1,) + shape[1:],
            lambda b, c, pt: (pt[b * n_pages + c * pps + pg], 0, 0, 0))

    def seq_spec(shape):
        return pl.BlockSpec((1,) + shape[1:], lambda b, c, pt: (b,) + (0,) * (len(shape) - 1))

    grid_spec = pltpu.PrefetchScalarGridSpec(
        num_scalar_prefetch=1,
        grid=(db, n_steps),
        in_specs=[seq_spec(q3.shape), seq_spec(k_new3.shape), seq_spec(v_new3.shape),
                  pl.BlockSpec(lamv.shape, lambda b, c, pt: (0, 0)),
                  pl.BlockSpec(g.shape, lambda b, c, pt: (0, 0))]
                 + [page_spec(cache_k.shape, pg) for pg in range(pps)]
                 + [page_spec(cache_v.shape, pg) for pg in range(pps)],
        out_specs=seq_spec((db, ds, d)),
        scratch_shapes=[
            pltpu.VMEM((N_HEADS, 2 * ds, 1), jnp.float32),
            pltpu.VMEM((N_HEADS, 2 * ds, 1), jnp.float32),
            pltpu.VMEM((N_HEADS, 2 * ds, V_DIM), jnp.float32),
        ],
    )
    return pl.pallas_call(
        functools.partial(_sample_attn_body, n_steps=n_steps),
        grid_spec=grid_spec,
        out_shape=jax.ShapeDtypeStruct((db, ds, d), jnp.float32),
        compiler_params=_params(2),
        name="sample_attn",
    )(page_table.reshape(-1), q3, k_new3, v_new3, lamv, g,
      *([cache_k] * pps), *([cache_v] * pps))


def _proj_ln_body(o_ref, x_ref, w_ref, g_ref, b_ref, y_ref):
    y = jnp.dot(_bf16(o_ref[...]), w_ref[...], preferred_element_type=jnp.float32)
    y_ref[...] = _layernorm(ALPHA * x_ref[...] + y, g_ref[...], b_ref[...])


def _proj_ln(o, x, w, g, b, o_tile=lambda i: i):
    n, d = x.shape
    tm = min(TOKEN_TILE, n)
    row = pl.BlockSpec((tm, d), lambda i: (i, 0))
    return pl.pallas_call(
        _proj_ln_body,
        grid=(n // tm,),
        in_specs=[pl.BlockSpec((tm, d), lambda i: (o_tile(i), 0)),
                  row, _const_spec(w.shape), _const_spec(g.shape), _const_spec(b.shape)],
        out_specs=row,
        out_shape=jax.ShapeDtypeStruct((n, d), jnp.float32),
        compiler_params=_params(1),
        name="proj_ln",
    )(o, x, w, g, b)


def kernel(x_prompt, x_sample, state_conv, cache_k, cache_v, page_table, w_conv_in, w_conv, w_conv_out, w_qkv, lambda_q1, lambda_k1, lambda_q2, lambda_k2, subln_g, w_attn_out, ln_mix_g, ln_mix_b, w_ffn_in, w_ffn_out, ln_ffn_g, ln_ffn_b):
    b, s, d = x_prompt.shape
    db, ds, _ = x_sample.shape
    n_pool, page = cache_k.shape[:2]
    past_len = page_table.shape[1] * page
    f32 = jnp.float32
    assert ds == SUBLANES and s % TOKEN_TILE == 0

    w_conv_in_b, w_conv_out_b = _bf16(w_conv_in), _bf16(w_conv_out)
    w_qkv_b, w_attn_out_b = _bf16(w_qkv), _bf16(w_attn_out)
    w_ffn_in_b, w_ffn_out_b = _bf16(w_ffn_in), _bf16(w_ffn_out)
    taps = w_conv.astype(f32)
    row = lambda a: a.reshape(1, -1).astype(f32)
    lamv = jnp.stack([lambda_q1, lambda_k1, lambda_q2, lambda_k2]).astype(f32)
    g_sub = row(subln_g)

    xp = x_prompt.reshape(b * s, d)
    xs = x_sample.reshape(db * ds, d)

    i = 0
    xp, tail_p = _conv_prompt(xp, w_conv_in_b, taps, w_conv_out_b,
                              row(ln_mix_g[i]), row(ln_mix_b[i]), s)
    conv_p = tail_p.reshape(b, SUBLANES, d)[:, SUBLANES - (CONV_WIDTH - 1):]
    st = jnp.pad(state_conv, ((0, 0), (0, ds - (CONV_WIDTH - 1)), (0, 0))).reshape(db * ds, d)
    xs, u_s = _conv_sample(xs, st, w_conv_in_b, taps, w_conv_out_b,
                           row(ln_mix_g[i]), row(ln_mix_b[i]))
    conv_s = u_s.reshape(db, ds, d)[:, ds - (CONV_WIDTH - 1):]
    xp = _ffn(xp, w_ffn_in_b[i], w_ffn_out_b[i], row(ln_ffn_g[i]), row(ln_ffn_b[i]))
    xs = _ffn(xs, w_ffn_in_b[i], w_ffn_out_b[i], row(ln_ffn_g[i]), row(ln_ffn_b[i]))

    i = 1
    tab_p = _rope_tables(jnp.arange(s))
    tab_s = _rope_tables(past_len + jnp.arange(db * ds) % ds)
    qt_p, kb_p, kt_p, vt_p, v_p = _qkv_prompt(xp, w_qkv_b, tab_p, s)
    qs, k_s, v_s = _qkv_sample(xs, w_qkv_b, tab_s)
    k_p = kt_p.reshape(b, 2 * N_HEADS, HEAD_DIM, s).transpose(0, 3, 1, 2)

    op = _prompt_attn(qt_p, kb_p.reshape(b, s, d),
                      vt_p.reshape(b, s // ATTN_KV_TILE, d, ATTN_KV_TILE),
                      lamv, g_sub.reshape(-1, 1))
    heads_first = lambda a, nh: a.reshape(db, ds, nh, d // nh).transpose(0, 2, 1, 3)
    os_ = _sample_attn(heads_first(qs, 2 * N_HEADS), heads_first(k_s, 2 * N_HEADS),
                       heads_first(v_s, N_HEADS), cache_k.transpose(0, 2, 3, 1), cache_v,
                       page_table, lamv, g_sub)
    nq = s // ATTN_Q_TILE
    assert ATTN_Q_TILE == TOKEN_TILE
    xp = _proj_ln(op.reshape(b * s, d), xp, w_attn_out_b, row(ln_mix_g[i]), row(ln_mix_b[i]),
                  o_tile=lambda r: r // nq * nq + _paired_tile_slot(r % nq, nq))
    xs = _proj_ln(os_.reshape(db * ds, d), xs, w_attn_out_b, row(ln_mix_g[i]), row(ln_mix_b[i]))
    xp = _ffn(xp, w_ffn_in_b[i], w_ffn_out_b[i], row(ln_ffn_g[i]), row(ln_ffn_b[i]))
    xs = _ffn(xs, w_ffn_in_b[i], w_ffn_out_b[i], row(ln_ffn_g[i]), row(ln_ffn_b[i]))

    return (xp.reshape(b, s, d), xs.reshape(db, ds, d), conv_p,
            k_p, v_p.reshape(b, s, N_HEADS, V_DIM),
            conv_s,
            k_s.reshape(db, ds, 2 * N_HEADS, HEAD_DIM), v_s.reshape(db, ds, N_HEADS, V_DIM))
```

```python
import functools
import math
from typing import NamedTuple

import jax
import jax.numpy as jnp
from jax import lax
from jax.experimental import pallas as pl
from jax.experimental.pallas import tpu as pltpu

N_HEADS = 8
HEAD_DIM = 64
V_DIM = 2 * HEAD_DIM
ROT_DIM = HEAD_DIM // 4
ROPE_THETA = 500000.0
CONV_WIDTH = 3
DEPTH = 2
LN_EPS = 1e-5
SUBLN_EPS = 1e-5
ALPHA = (2 * DEPTH) ** 0.25
SCALE = HEAD_DIM ** -0.5
LOG2_E = math.log2(math.e)
ATTN_LAYER = 1
LAM_INIT = 0.8 - 0.6 * math.exp(-0.3 * ATTN_LAYER)

LANES = 128
SUBLANES = 8
BF16_SUBLANES = 16
MXU_DIM = 256
VMEM_LIMIT_BYTES = 56 * 1024 * 1024

TOKEN_TILE = 512
SUB_TILE = 256
ATTN_Q_TILE = 512
ATTN_KV_TILE = 512
PAGES_PER_STEP = 8

_NT = (((1,), (1,)), ((), ()))


def _bf16(x):
    return x.astype(jnp.bfloat16)


def _layernorm(y, g, b):
    mu = jnp.mean(y, axis=-1, keepdims=True)
    yc = y - mu
    var = jnp.mean(yc * yc, axis=-1, keepdims=True)
    return yc * lax.rsqrt(var + LN_EPS) * g + b


def _const_spec(shape):
    nd = len(shape)
    return pl.BlockSpec(shape, lambda *_: (0,) * nd, pipeline_mode=pl.Buffered(1))


def _params(n_axes):
    return pltpu.CompilerParams(
        dimension_semantics=("arbitrary",) * n_axes,
        vmem_limit_bytes=VMEM_LIMIT_BYTES)


def _sub_tiles(rows):
    step = min(rows, SUB_TILE)
    return [slice(r, r + step) for r in range(0, rows, step)]


def _conv_prompt_body(x_ref, w_in_ref, taps_ref, w_out_ref, g_ref, b_ref,
                      o_ref, tail_ref, carry_ref, *, tiles_per_seq):
    i = pl.program_id(0)
    d = x_ref.shape[1]

    @pl.when(i % tiles_per_seq == 0)
    def _():
        carry_ref[...] = jnp.zeros_like(carry_ref)

    taps = taps_ref[...]
    prev = carry_ref[...]
    for rows in _sub_tiles(x_ref.shape[0]):
        x = x_ref[rows, :]
        h3 = jnp.dot(_bf16(x), w_in_ref[...], preferred_element_type=jnp.float32)
        gb, gc, h = h3[:, :d], h3[:, d:2 * d], h3[:, 2 * d:]
        u = gc * h
        row = lax.broadcasted_iota(jnp.int32, (x.shape[0], 1), 0)
        c6 = prev[SUBLANES - 2:SUBLANES - 1, :]
        c7 = prev[SUBLANES - 1:SUBLANES, :]
        u1 = jnp.where(row == 0, c7, pltpu.roll(u, 1, 0))
        u2 = jnp.where(row == 0, c6, jnp.where(row == 1, c7, pltpu.roll(u, 2, 0)))
        conv = taps[0:1, :] * u2 + taps[1:2, :] * u1 + taps[2:3, :] * u
        y = jnp.dot(_bf16(gb * conv), w_out_ref[...], preferred_element_type=jnp.float32)
        o_ref[rows, :] = _layernorm(ALPHA * x + y, g_ref[...], b_ref[...])
        prev = u[x.shape[0] - SUBLANES:, :]

    carry_ref[...] = prev

    @pl.when(i % tiles_per_seq == tiles_per_seq - 1)
    def _():
        tail_ref[...] = prev


def _conv_prompt(x, w_in, taps, w_out, g, b, seq_len):
    n, d = x.shape
    tm = TOKEN_TILE
    tiles_per_seq = seq_len // tm
    return pl.pallas_call(
        functools.partial(_conv_prompt_body, tiles_per_seq=tiles_per_seq),
        grid=(n // tm,),
        in_specs=[
            pl.BlockSpec((tm, d), lambda i: (i, 0)),
            _const_spec(w_in.shape), _const_spec(taps.shape), _const_spec(w_out.shape),
            _const_spec(g.shape), _const_spec(b.shape),
        ],
        out_specs=[
            pl.BlockSpec((tm, d), lambda i: (i, 0)),
            pl.BlockSpec((SUBLANES, d), lambda i: (i // tiles_per_seq, 0)),
        ],
        out_shape=[
            jax.ShapeDtypeStruct((n, d), jnp.float32),
            jax.ShapeDtypeStruct((n // seq_len * SUBLANES, d), jnp.float32),
        ],
        scratch_shapes=[pltpu.VMEM((SUBLANES, d), jnp.float32)],
        compiler_params=_params(1),
        name="conv_prompt",
    )(x, w_in, taps, w_out, g, b)


def _conv_sample_body(x_ref, st_ref, w_in_ref, taps_ref, w_out_ref, g_ref, b_ref,
                      o_ref, u_ref):
    x = x_ref[...]
    t, d = x.shape
    h3 = jnp.dot(_bf16(x), w_in_ref[...], preferred_element_type=jnp.float32)
    gb, gc, h = h3[:, :d], h3[:, d:2 * d], h3[:, 2 * d:]
    u = gc * h
    st = st_ref[...]
    pos = lax.broadcasted_iota(jnp.int32, (t, 1), 0) % SUBLANES
    u1 = jnp.where(pos == 0, pltpu.roll(st, t - 1, 0), pltpu.roll(u, 1, 0))
    u2 = jnp.where(pos < 2, st, pltpu.roll(u, 2, 0))
    taps = taps_ref[...]
    conv = taps[0:1, :] * u2 + taps[1:2, :] * u1 + taps[2:3, :] * u
    y = jnp.dot(_bf16(gb * conv), w_out_ref[...], preferred_element_type=jnp.float32)
    o_ref[...] = _layernorm(ALPHA * x + y, g_ref[...], b_ref[...])
    u_ref[...] = u


def _conv_sample(x, st, w_in, taps, w_out, g, b):
    n, d = x.shape
    return pl.pallas_call(
        _conv_sample_body,
        grid=(1,),
        in_specs=[_const_spec(a.shape) for a in (x, st, w_in, taps, w_out, g, b)],
        out_specs=[_const_spec((n, d)), _const_spec((n, d))],
        out_shape=[jax.ShapeDtypeStruct((n, d), jnp.float32)] * 2,
        compiler_params=_params(1),
        name="conv_sample",
    )(x, st, w_in, taps, w_out, g, b)


def _ffn_halves(w_ffn_in, w_ffn_out):
    depth, d, f2 = w_ffn_in.shape
    f = f2 // 2
    assert f % (2 * LANES) == 0
    gate = w_ffn_in[..., :f].reshape(depth, d, 2, f // 2)
    up = w_ffn_in[..., f:].reshape(depth, d, 2, f // 2)
    w_in_h = jnp.concatenate([gate, up], axis=-1).transpose(0, 2, 1, 3)
    return _bf16(w_in_h), _bf16(w_ffn_out.reshape(depth, 2, f // 2, d))


def _ffn_half(x, w_in, w_out):
    fh = w_out.shape[0]
    h = jnp.dot(_bf16(x), w_in, preferred_element_type=jnp.float32)
    gate, up = h[:, :fh], h[:, fh:]
    a = gate * jax.nn.sigmoid(gate) * up
    return jnp.dot(_bf16(a), w_out, preferred_element_type=jnp.float32)


def _ffn_sample_body(x_ref, w_in_ref, w_out_ref, g_ref, b_ref, o_ref):
    x = x_ref[...]
    y = (_ffn_half(x, w_in_ref[0, 0], w_out_ref[0, 0])
         + _ffn_half(x, w_in_ref[0, 1], w_out_ref[0, 1]))
    o_ref[...] = _layernorm(ALPHA * x + y, g_ref[...], b_ref[...])


def _layer_spec(shape, layer):
    nd = len(shape)
    return pl.BlockSpec((1,) + shape[1:], lambda *_: (layer,) + (0,) * (nd - 1),
                        pipeline_mode=pl.Buffered(1))


def _ffn_sample(x, w_in_h, w_out_h, layer, g, b):
    n, d = x.shape
    return pl.pallas_call(
        _ffn_sample_body,
        grid=(1,),
        in_specs=[_const_spec(x.shape), _layer_spec(w_in_h.shape, layer),
                  _layer_spec(w_out_h.shape, layer), _const_spec(g.shape), _const_spec(b.shape)],
        out_specs=_const_spec((n, d)),
        out_shape=jax.ShapeDtypeStruct((n, d), jnp.float32),
        compiler_params=_params(1),
        name="ffn_sample",
    )(x, w_in_h, w_out_h, g, b)


def _rope_tables(pos):
    half = ROT_DIM // 2
    inv = jnp.power(ROPE_THETA, -jnp.arange(0, ROT_DIM, 2, dtype=jnp.float32) / ROT_DIM)
    ang = pos.astype(jnp.float32)[:, None] * inv[None, :]
    cos, sin = jnp.cos(ang), jnp.sin(ang)
    dd = jnp.arange(LANES) % HEAD_DIM
    cos_l = jnp.take(cos, dd % half, axis=1)
    sin_l = jnp.take(sin, dd % half, axis=1)
    c = jnp.where(dd[None, :] < ROT_DIM, cos_l, 1.0)
    s_up = jnp.where(dd[None, :] < half, -sin_l, 0.0)
    s_dn = jnp.where((dd[None, :] >= half) & (dd[None, :] < ROT_DIM), sin_l, 0.0)
    return c, s_up, s_dn


def _rope(x, c, s_up, s_dn):
    half = ROT_DIM // 2
    outs = []
    for g in range(x.shape[1] // LANES):
        xg = x[:, g * LANES:(g + 1) * LANES]
        x_up = pltpu.roll(xg, LANES - half, 1)
        x_dn = pltpu.roll(xg, half, 1)
        outs.append(xg * c + x_up * s_up + x_dn * s_dn)
    return jnp.concatenate(outs, axis=1)


def _qkv_rows(x_ref, w_ref, c_ref, su_ref, sd_ref, rows=slice(None)):
    x = x_ref[rows, :]
    d = x.shape[1]
    h3 = jnp.dot(_bf16(x), w_ref[...], preferred_element_type=jnp.float32)
    c, su, sd = c_ref[rows, :], su_ref[rows, :], sd_ref[rows, :]
    q = _rope(h3[:, :d], c, su, sd)
    k = _rope(h3[:, d:2 * d], c, su, sd)
    return q, k, h3[:, 2 * d:]


def _qkv_sample_body(x_ref, w_ref, c_ref, su_ref, sd_ref, q_ref, k_ref, v_ref):
    q, k, v = _qkv_rows(x_ref, w_ref, c_ref, su_ref, sd_ref)
    q_ref[...] = q * SCALE
    k_ref[...] = k
    v_ref[...] = v


def _qkv_sample(x, w, tables):
    n, d = x.shape
    specs = [_const_spec(a.shape) for a in (x, w) + tuple(tables)]
    return pl.pallas_call(
        _qkv_sample_body,
        grid=(1,),
        in_specs=specs,
        out_specs=[_const_spec((n, d))] * 3,
        out_shape=[jax.ShapeDtypeStruct((n, d), jnp.float32)] * 3,
        compiler_params=_params(1),
        name="qkv_sample",
    )(x, w, *tables)


def _qkv_prompt_body(x_ref, w_ref, c_ref, su_ref, sd_ref,
                     qt_ref, kb_ref, kt_ref, vt_ref, v_ref):
    for rows in _sub_tiles(x_ref.shape[0]):
        q, k, v = _qkv_rows(x_ref, w_ref, c_ref, su_ref, sd_ref, rows)
        qt_ref[0, :, rows] = _bf16((q * (SCALE * LOG2_E)).T)
        kb_ref[rows, :] = _bf16(k)
        kt_ref[0, :, rows] = k.T
        vt_ref[0, :, rows] = _bf16(v.T)
        v_ref[rows, :] = v


def _qkv_prompt(x, w, tables, seq_len):
    n, d = x.shape
    tm = TOKEN_TILE
    tps = seq_len // tm
    tspec = pl.BlockSpec((tm, LANES), lambda i: (i % tps, 0))
    row = pl.BlockSpec((tm, d), lambda i: (i, 0))
    blk = pl.BlockSpec((1, d, tm), lambda i: (i, 0, 0))
    return pl.pallas_call(
        _qkv_prompt_body,
        grid=(n // tm,),
        in_specs=[row, _const_spec(w.shape), tspec, tspec, tspec],
        out_specs=[blk, row, pl.BlockSpec((1, d, tm), lambda i: (i // tps, 0, i % tps)), blk, row],
        out_shape=[
            jax.ShapeDtypeStruct((n // tm, d, tm), jnp.bfloat16),
            jax.ShapeDtypeStruct((n, d), jnp.bfloat16),
            jax.ShapeDtypeStruct((n // seq_len, d, seq_len), jnp.float32),
            jax.ShapeDtypeStruct((n // tm, d, tm), jnp.bfloat16),
            jax.ShapeDtypeStruct((n, d), jnp.float32),
        ],
        compiler_params=_params(1),
        name="qkv_prompt",
    )(x, w, *tables)


def _lambda(lamv_ref):
    lv = lamv_ref[...]
    d1 = jnp.sum(lv[0:1, :] * lv[1:2, :], axis=-1, keepdims=True)
    d2 = jnp.sum(lv[2:3, :] * lv[3:4, :], axis=-1, keepdims=True)
    return jnp.exp(d1) - jnp.exp(d2) + LAM_INIT


def _subln(o, g):
    o = o * lax.rsqrt(jnp.mean(o * o, axis=-1, keepdims=True) + SUBLN_EPS)
    return o * g * (1.0 - LAM_INIT)


def _prompt_attn_body(qlo_ref, qhi_ref, k_ref, vt_ref, lamv_ref, gcol_ref, o_ref,
                      q_ref, s0_ref, s1_ref, bmax_ref, m_ref, acc_ref, *, n_tiles):
    tq, tk = ATTN_Q_TILE, ATTN_KV_TILE
    t_lo = pl.program_id(2)
    t_hi = n_tiles - 1 - t_lo
    neg = jnp.finfo(jnp.float32).min
    feat = lax.broadcasted_iota(jnp.int32, (2 * HEAD_DIM, 1), 0)
    for tile, ref in enumerate((qlo_ref, qhi_ref)):
        q_ref[tile, 0] = jnp.where(feat < HEAD_DIM, ref[0], 0)
        q_ref[tile, 1] = jnp.where(feat >= HEAD_DIM, ref[0], 0)
    ones = jnp.ones((acc_ref.shape[2] - V_DIM, tk), jnp.bfloat16)
    acc_ref[...] = jnp.zeros_like(acc_ref)
    m_ref[...] = jnp.full_like(m_ref, neg)
    s_refs = (s0_ref, s1_ref)

    def block(u):
        if u == 0:
            return 0, t_lo, True
        if u == 1:
            return 1, t_hi, True
        n = u - 2
        return jnp.where(n < t_lo, 0, 1), jnp.where(n < t_lo, n, n - t_lo), False

    def scores(u):
        tile, kv, masked = block(u)
        kblk = k_ref[0, pl.ds(pl.multiple_of(kv * tk, tk), tk), :]
        for c in range(2):
            st = jnp.dot(kblk, q_ref[tile, c], preferred_element_type=jnp.float32)
            if masked:
                key = lax.broadcasted_iota(jnp.int32, (tk, tq), 0)
                qry = lax.broadcasted_iota(jnp.int32, (tk, tq), 1)
                st = jnp.where(key <= qry, st, neg)
            s_refs[u % 2][c] = st
            bmax_ref[u % 2, c] = jnp.max(st, axis=0, keepdims=True)

    def accumulate(u):
        tile, kv, _ = block(u)
        s_ref = s_refs[u % 2]
        vext = jnp.concatenate([vt_ref[0, kv], ones], axis=0)
        for c in range(2):
            m_old = m_ref[tile, c]
            m_new = jnp.maximum(m_old, bmax_ref[u % 2, c])
            alpha = jnp.exp2(m_old - m_new)
            m_ref[tile, c] = m_new
            p = _bf16(jnp.exp2(s_ref[c] - m_new))
            acc_ref[tile, c] = alpha * acc_ref[tile, c] + jnp.dot(
                vext, p, preferred_element_type=jnp.float32)

    n_blocks = n_tiles + 1
    scores(0)
    for u in range(n_blocks):
        if u + 1 < n_blocks:
            scores(u + 1)
        accumulate(u)

    lam = _lambda(lamv_ref)
    for tile in range(2):
        o = (acc_ref[tile, 0, :V_DIM] / acc_ref[tile, 0, V_DIM:V_DIM + 1]
             - lam * (acc_ref[tile, 1, :V_DIM] / acc_ref[tile, 1, V_DIM:V_DIM + 1]))
        o = o * lax.rsqrt(jnp.mean(o * o, axis=0, keepdims=True) + SUBLN_EPS)
        o = o * gcol_ref[...] * (1.0 - LAM_INIT)
        o_ref[0, tile, 0] = o.T.astype(o_ref.dtype)


def _prompt_attn(qt, k, vt, lamv, gcol):
    b, s, d = k.shape
    tq, tk = ATTN_Q_TILE, ATTN_KV_TILE
    nq = s // tq
    assert tq == tk and s % tq == 0 and nq % 2 == 0
    return pl.pallas_call(
        functools.partial(_prompt_attn_body, n_tiles=nq),
        grid=(b, N_HEADS, nq // 2),
        in_specs=[
            pl.BlockSpec((1, V_DIM, tq), lambda bi, j, t: (bi * nq + t, j, 0)),
            pl.BlockSpec((1, V_DIM, tq), lambda bi, j, t: (bi * nq + nq - 1 - t, j, 0)),
            pl.BlockSpec((1, s, V_DIM), lambda bi, j, t: (bi, 0, j)),
            pl.BlockSpec((1, s // tk, V_DIM, tk), lambda bi, j, t: (bi, 0, j, 0)),
            _const_spec(lamv.shape), _const_spec(gcol.shape),
        ],
        out_specs=pl.BlockSpec((1, 2, 1, tq, V_DIM), lambda bi, j, t: (bi, 0, t, 0, j)),
        out_shape=jax.ShapeDtypeStruct((b, 2, nq // 2, tq, d), jnp.bfloat16),
        scratch_shapes=[
            pltpu.VMEM((2, 2, V_DIM, tq), jnp.bfloat16),
            pltpu.VMEM((2, tk, tq), jnp.float32),
            pltpu.VMEM((2, tk, tq), jnp.float32),
            pltpu.VMEM((2, 2, 1, tq), jnp.float32),
            pltpu.VMEM((2, 2, 1, tq), jnp.float32),
            pltpu.VMEM((2, 2, V_DIM + BF16_SUBLANES, tq), jnp.float32),
        ],
        compiler_params=_params(3),
        name="prompt_attn",
    )(qt, qt, k, vt, lamv, gcol)


def _paired_tile_slot(t, nq):
    return jnp.where(t < nq // 2, t, nq // 2 + nq - 1 - t)


class _DecodeRefs(NamedTuple):
    q: object
    kn: object
    vn: object
    lamv: object
    g: object
    k_pages: tuple
    v_pages: tuple
    out: object
    m: object
    l: object
    acc: object


def _decode_math(r):
    ds = r.q.shape[2]
    q3 = _bf16(r.q[0])

    def pair_rows(s):
        return s.reshape(N_HEADS, 2 * ds, s.shape[2])

    def scores(k3):
        return pair_rows(jnp.einsum("hqd,htd->hqt", q3, k3, preferred_element_type=jnp.float32))

    def scores_t(kt3):
        return pair_rows(jnp.einsum("hqd,hdt->hqt", q3, kt3, preferred_element_type=jnp.float32))

    def weighted(p, v3):
        return jnp.einsum("jrt,jte->jre", _bf16(p), v3, preferred_element_type=jnp.float32)

    return ds, scores, scores_t, weighted


def _decode_init(r):
    ds, scores, _, weighted = _decode_math(r)
    page = r.v_pages[0].shape[1]
    kn, vn = r.kn[0], r.vn[0]
    kn3 = _bf16(jnp.concatenate(
        [kn, jnp.zeros((kn.shape[0], page - ds, kn.shape[2]), jnp.float32)], axis=1))
    vn3 = _bf16(jnp.concatenate(
        [vn, jnp.zeros((vn.shape[0], page - ds, vn.shape[2]), jnp.float32)], axis=1))
    s = scores(kn3)
    qi = lax.broadcasted_iota(jnp.int32, s.shape, 1) % ds
    tt = lax.broadcasted_iota(jnp.int32, s.shape, 2)
    s = jnp.where(tt <= qi, s, jnp.finfo(jnp.float32).min)
    m = jnp.max(s, axis=-1, keepdims=True)
    p = jnp.exp(s - m)
    r.m[...] = m
    r.l[...] = jnp.sum(p, axis=-1, keepdims=True)
    r.acc[...] = weighted(p, vn3)


def _decode_pages(r):
    _, _, scores_t, weighted = _decode_math(r)
    v_pages = []
    for ref in r.v_pages:
        v_pages.append(_bf16(pltpu.einshape("tje->jte", ref[0])))
        yield
    kt_pages = []
    for n, ref in enumerate(r.k_pages):
        kt_pages.append(_bf16(ref[0]))
        if n % 2:
            yield
    s = scores_t(jnp.concatenate(kt_pages, axis=2))
    yield
    m_old = r.m[...]
    m_new = jnp.maximum(m_old, jnp.max(s, axis=-1, keepdims=True))
    alpha = jnp.exp(m_old - m_new)
    p = jnp.exp(s - m_new)
    r.l[...] = alpha * r.l[...] + jnp.sum(p, axis=-1, keepdims=True)
    r.m[...] = m_new
    yield
    r.acc[...] = alpha * r.acc[...] + weighted(p, jnp.concatenate(v_pages, axis=1))


def _decode_finish(r):
    ds = r.q.shape[2]
    o = r.acc[...] / r.l[...]
    o = o[:, :ds, :] - _lambda(r.lamv) * o[:, ds:, :]
    o = _subln(o, r.g[...])
    for j in range(N_HEADS):
        r.out[0, :, j * V_DIM:(j + 1) * V_DIM] = o[j]


def _ffn_decode_body(pt_ref, x_ref, w_in_ref, w_out_ref, g_ref, b_ref,
                     q_ref, kn_ref, vn_ref, lamv_ref, gsub_ref, *refs, chunks_per_seq):
    pps = PAGES_PER_STEP
    o_ref, od_ref, y_ref, m_ref, l_ref, acc_ref = refs[2 * pps:]
    dec = _DecodeRefs(q_ref, kn_ref, vn_ref, lamv_ref, gsub_ref,
                      refs[:pps], refs[pps:2 * pps], od_ref, m_ref, l_ref, acc_ref)
    s = pl.program_id(0)
    half = s % 2
    chunk = s % chunks_per_seq

    @pl.when(s == 0)
    def _():
        y_ref[...] = jnp.zeros_like(y_ref)

    @pl.when(chunk == 0)
    def _():
        _decode_init(dec)

    def ffn_piece():
        x = x_ref[...]
        xb = _bf16(x)
        fh = w_out_ref.shape[2]
        cols = w_in_ref.shape[3]
        step = 2 * MXU_DIM
        hs = []
        for c0 in range(0, cols, step):
            hs.append(jnp.dot(xb, w_in_ref[0, half, :, c0:min(c0 + step, cols)],
                              preferred_element_type=jnp.float32))
            yield
        h = jnp.concatenate(hs, axis=1)
        gate, up = h[:, :fh], h[:, fh:]
        a = _bf16(gate * jax.nn.sigmoid(gate) * up)
        yield
        y = jnp.dot(a, w_out_ref[0, half], preferred_element_type=jnp.float32)
        y = jnp.where(half == 0, y, y_ref[...] + y)
        y_ref[...] = y
        yield
        o_ref[...] = _layernorm(ALPHA * x + y, g_ref[...], b_ref[...])

    ffn, decode = ffn_piece(), _decode_pages(dec)
    live = {ffn, decode}
    while live:
        for stream in (ffn, decode, decode):
            if stream in live and next(stream, live) is live:
                live.discard(stream)

    @pl.when(chunk == chunks_per_seq - 1)
    def _():
        _decode_finish(dec)


def _ffn_decode(x, w_in_h, w_out_h, layer, g, b,
                q3, k_new3, v_new3, cache_kt, cache_v, page_table, lamv, gsub):
    n, d = x.shape
    sb, nh2, ds, hd = q3.shape
    n_pages = page_table.shape[1]
    pps = PAGES_PER_STEP
    chunks = n_pages // pps
    rows = SUB_TILE
    n_steps = 2 * (n // rows)
    assert n_pages % pps == 0 and ds == SUBLANES and n_steps == sb * chunks

    def page_spec(shape, pg):
        return pl.BlockSpec(
            (1,) + shape[1:],
            lambda s, pt: (pt[s // chunks * n_pages + s % chunks * pps + pg], 0, 0, 0))

    def seq_spec(shape):
        return pl.BlockSpec((1,) + shape[1:],
                            lambda s, pt: (s // chunks,) + (0,) * (len(shape) - 1))

    row_spec = pl.BlockSpec((rows, d), lambda s, pt: (s // 2, 0))
    grid_spec = pltpu.PrefetchScalarGridSpec(
        num_scalar_prefetch=1,
        grid=(n_steps,),
        in_specs=[row_spec, _layer_spec(w_in_h.shape, layer), _layer_spec(w_out_h.shape, layer),
                  _const_spec(g.shape), _const_spec(b.shape),
                  seq_spec(q3.shape), seq_spec(k_new3.shape), seq_spec(v_new3.shape),
                  _const_spec(lamv.shape), _const_spec(gsub.shape)]
                 + [page_spec(cache_kt.shape, pg) for pg in range(pps)]
                 + [page_spec(cache_v.shape, pg) for pg in range(pps)],
        out_specs=[row_spec, seq_spec((sb, ds, d))],
        scratch_shapes=[
            pltpu.VMEM((rows, d), jnp.float32),
            pltpu.VMEM((N_HEADS, 2 * ds, 1), jnp.float32),
            pltpu.VMEM((N_HEADS, 2 * ds, 1), jnp.float32),
            pltpu.VMEM((N_HEADS, 2 * ds, V_DIM), jnp.float32),
        ],
    )
    return pl.pallas_call(
        functools.partial(_ffn_decode_body, chunks_per_seq=chunks),
        grid_spec=grid_spec,
        out_shape=[jax.ShapeDtypeStruct((n, d), jnp.float32),
                   jax.ShapeDtypeStruct((sb, ds, d), jnp.float32)],
        compiler_params=_params(1),
        name="ffn_decode",
    )(page_table.reshape(-1), x, w_in_h, w_out_h, g, b, q3, k_new3, v_new3, lamv, gsub,
      *([cache_kt] * pps), *([cache_v] * pps))


def _proj_ln_body(o_ref, x_ref, w_ref, g_ref, b_ref, y_ref):
    for rows in _sub_tiles(x_ref.shape[0]):
        y = jnp.dot(_bf16(o_ref[rows, :]), w_ref[...], preferred_element_type=jnp.float32)
        y_ref[rows, :] = _layernorm(ALPHA * x_ref[rows, :] + y, g_ref[...], b_ref[...])


def _proj_ln(o, x, w, g, b, o_tile=lambda i: i):
    n, d = x.shape
    tm = min(TOKEN_TILE, n)
    row = pl.BlockSpec((tm, d), lambda i: (i, 0))
    return pl.pallas_call(
        _proj_ln_body,
        grid=(n // tm,),
        in_specs=[pl.BlockSpec((tm, d), lambda i: (o_tile(i), 0)),
                  row, _const_spec(w.shape), _const_spec(g.shape), _const_spec(b.shape)],
        out_specs=row,
        out_shape=jax.ShapeDtypeStruct((n, d), jnp.float32),
        compiler_params=_params(1),
        name="proj_ln",
    )(o, x, w, g, b)


def kernel(x_prompt, x_sample, state_conv, cache_k, cache_v, page_table, w_conv_in, w_conv, w_conv_out, w_qkv, lambda_q1, lambda_k1, lambda_q2, lambda_k2, subln_g, w_attn_out, ln_mix_g, ln_mix_b, w_ffn_in, w_ffn_out, ln_ffn_g, ln_ffn_b):
    b, s, d = x_prompt.shape
    db, ds, _ = x_sample.shape
    n_pool, page = cache_k.shape[:2]
    past_len = page_table.shape[1] * page
    f32 = jnp.float32
    assert ds == SUBLANES and s % TOKEN_TILE == 0

    w_conv_in_b, w_conv_out_b = _bf16(w_conv_in), _bf16(w_conv_out)
    w_qkv_b, w_attn_out_b = _bf16(w_qkv), _bf16(w_attn_out)
    w_ffn_in_h, w_ffn_out_h = _ffn_halves(w_ffn_in, w_ffn_out)
    taps = w_conv.astype(f32)
    row = lambda a: a.reshape(1, -1).astype(f32)
    lamv = jnp.stack([lambda_q1, lambda_k1, lambda_q2, lambda_k2]).astype(f32)
    g_sub = row(subln_g)

    xp = x_prompt.reshape(b * s, d)
    xs = x_sample.reshape(db * ds, d)

    i = 0
    xp, tail_p = _conv_prompt(xp, w_conv_in_b, taps, w_conv_out_b,
                              row(ln_mix_g[i]), row(ln_mix_b[i]), s)
    conv_p = tail_p.reshape(b, SUBLANES, d)[:, SUBLANES - (CONV_WIDTH - 1):]
    st = jnp.pad(state_conv, ((0, 0), (0, ds - (CONV_WIDTH - 1)), (0, 0))).reshape(db * ds, d)
    xs, u_s = _conv_sample(xs, st, w_conv_in_b, taps, w_conv_out_b,
                           row(ln_mix_g[i]), row(ln_mix_b[i]))
    conv_s = u_s.reshape(db, ds, d)[:, ds - (CONV_WIDTH - 1):]
    xs = _ffn_sample(xs, w_ffn_in_h, w_ffn_out_h, i, row(ln_ffn_g[i]), row(ln_ffn_b[i]))

    tab_s = _rope_tables(past_len + jnp.arange(db * ds) % ds)
    qs, k_s, v_s = _qkv_sample(xs, w_qkv_b, tab_s)
    heads_first = lambda a, nh: a.reshape(db, ds, nh, d // nh).transpose(0, 2, 1, 3)
    q3, k_new3, v_new3 = (heads_first(qs, 2 * N_HEADS), heads_first(k_s, 2 * N_HEADS),
                          heads_first(v_s, N_HEADS))
    cache_kt = cache_k.transpose(0, 2, 3, 1)
    hb = db // 2

    def ffn_decode(x, layer, seqs):
        return _ffn_decode(x, w_ffn_in_h, w_ffn_out_h, layer,
                           row(ln_ffn_g[layer]), row(ln_ffn_b[layer]),
                           q3[seqs], k_new3[seqs], v_new3[seqs], cache_kt, cache_v,
                           page_table[seqs], lamv, g_sub)

    xp, os_lo = ffn_decode(xp, i, slice(0, hb))

    i = 1
    tab_p = _rope_tables(jnp.arange(s))
    qt_p, kb_p, kt_p, vt_p, v_p = _qkv_prompt(xp, w_qkv_b, tab_p, s)
    k_p = kt_p.reshape(b, 2 * N_HEADS, HEAD_DIM, s).transpose(0, 3, 1, 2)

    op = _prompt_attn(qt_p, kb_p.reshape(b, s, d),
                      vt_p.reshape(b, s // ATTN_KV_TILE, d, ATTN_KV_TILE),
                      lamv, g_sub.reshape(-1, 1))
    nq = s // ATTN_Q_TILE
    assert ATTN_Q_TILE == TOKEN_TILE
    xp = _proj_ln(op.reshape(b * s, d), xp, w_attn_out_b, row(ln_mix_g[i]), row(ln_mix_b[i]),
                  o_tile=lambda r: r // nq * nq + _paired_tile_slot(r % nq, nq))
    xp, os_hi = ffn_decode(xp, i, slice(hb, db))
    os_ = jnp.concatenate([os_lo, os_hi], axis=0)
    xs = _proj_ln(os_.reshape(db * ds, d), xs, w_attn_out_b, row(ln_mix_g[i]), row(ln_mix_b[i]))
    xs = _ffn_sample(xs, w_ffn_in_h, w_ffn_out_h, i, row(ln_ffn_g[i]), row(ln_ffn_b[i]))

    return (xp.reshape(b, s, d), xs.reshape(db, ds, d), conv_p,
            k_p, v_p.reshape(b, s, N_HEADS, V_DIM),
            conv_s,
            k_s.reshape(db, ds, 2 * N_HEADS, HEAD_DIM), v_s.reshape(db, ds, N_HEADS, V_DIM))
```

```python
import functools
import math
from typing import NamedTuple

import jax
import jax.numpy as jnp
from jax import lax
from jax.experimental import pallas as pl
from jax.experimental.pallas import tpu as pltpu

N_HEADS = 8
HEAD_DIM = 64
V_DIM = 2 * HEAD_DIM
ROT_DIM = HEAD_DIM // 4
ROPE_THETA = 500000.0
CONV_WIDTH = 3
DEPTH = 2
LN_EPS = 1e-5
SUBLN_EPS = 1e-5
ALPHA = (2 * DEPTH) ** 0.25
SCALE = HEAD_DIM ** -0.5
LOG2_E = math.log2(math.e)
ATTN_LAYER = 1
LAM_INIT = 0.8 - 0.6 * math.exp(-0.3 * ATTN_LAYER)

LANES = 128
SUBLANES = 8
BF16_SUBLANES = 16
MXU_DIM = 256
VMEM_LIMIT_BYTES = 56 * 1024 * 1024

TOKEN_TILE = 512
SUB_TILE = 256
ATTN_Q_TILE = 512
ATTN_KV_TILE = 512
PAGES_PER_STEP = 8

_NT = (((1,), (1,)), ((), ()))


def _bf16(x):
    return x.astype(jnp.bfloat16)


def _layernorm(y, g, b):
    mu = jnp.mean(y, axis=-1, keepdims=True)
    yc = y - mu
    var = jnp.mean(yc * yc, axis=-1, keepdims=True)
    return yc * lax.rsqrt(var + LN_EPS) * g + b


def _const_spec(shape):
    nd = len(shape)
    return pl.BlockSpec(shape, lambda *_: (0,) * nd, pipeline_mode=pl.Buffered(1))


def _params(n_axes, flags=None):
    return pltpu.CompilerParams(
        dimension_semantics=("arbitrary",) * n_axes,
        vmem_limit_bytes=VMEM_LIMIT_BYTES,
        flags=flags)


def _sub_tiles(rows):
    step = min(rows, SUB_TILE)
    return [slice(r, r + step) for r in range(0, rows, step)]


def _conv_prompt_body(x_ref, w_in_ref, taps_ref, w_out_ref, g_ref, b_ref,
                      o_ref, tail_ref, carry_ref, *, tiles_per_seq):
    i = pl.program_id(0)
    d = x_ref.shape[1]

    @pl.when(i % tiles_per_seq == 0)
    def _():
        carry_ref[...] = jnp.zeros_like(carry_ref)

    x = x_ref[...]
    t = x.shape[0]
    h3 = jnp.dot(_bf16(x), w_in_ref[...], preferred_element_type=jnp.float32)
    gb, gc, h = h3[:, :d], h3[:, d:2 * d], h3[:, 2 * d:]
    u = gc * h
    row = lax.broadcasted_iota(jnp.int32, (t, 1), 0)
    c6 = carry_ref[SUBLANES - 2:SUBLANES - 1, :]
    c7 = carry_ref[SUBLANES - 1:SUBLANES, :]
    u1 = jnp.where(row == 0, c7, pltpu.roll(u, 1, 0))
    u2 = jnp.where(row == 0, c6, jnp.where(row == 1, c7, pltpu.roll(u, 2, 0)))
    taps = taps_ref[...]
    conv = taps[0:1, :] * u2 + taps[1:2, :] * u1 + taps[2:3, :] * u
    y = jnp.dot(_bf16(gb * conv), w_out_ref[...], preferred_element_type=jnp.float32)
    o_ref[...] = _layernorm(ALPHA * x + y, g_ref[...], b_ref[...])

    carry_ref[...] = u[t - SUBLANES:, :]

    @pl.when(i % tiles_per_seq == tiles_per_seq - 1)
    def _():
        tail_ref[...] = u[t - SUBLANES:, :]


def _conv_prompt(x, w_in, taps, w_out, g, b, seq_len):
    n, d = x.shape
    tm = TOKEN_TILE
    tiles_per_seq = seq_len // tm
    return pl.pallas_call(
        functools.partial(_conv_prompt_body, tiles_per_seq=tiles_per_seq),
        grid=(n // tm,),
        in_specs=[
            pl.BlockSpec((tm, d), lambda i: (i, 0)),
            _const_spec(w_in.shape), _const_spec(taps.shape), _const_spec(w_out.shape),
            _const_spec(g.shape), _const_spec(b.shape),
        ],
        out_specs=[
            pl.BlockSpec((tm, d), lambda i: (i, 0)),
            pl.BlockSpec((SUBLANES, d), lambda i: (i // tiles_per_seq, 0)),
        ],
        out_shape=[
            jax.ShapeDtypeStruct((n, d), jnp.float32),
            jax.ShapeDtypeStruct((n // seq_len * SUBLANES, d), jnp.float32),
        ],
        scratch_shapes=[pltpu.VMEM((SUBLANES, d), jnp.float32)],
        compiler_params=_params(1),
        name="conv_prompt",
    )(x, w_in, taps, w_out, g, b)


def _conv_sample_body(x_ref, st_ref, w_in_ref, taps_ref, w_out_ref, g_ref, b_ref,
                      o_ref, u_ref):
    x = x_ref[...]
    t, d = x.shape
    h3 = jnp.dot(_bf16(x), w_in_ref[...], preferred_element_type=jnp.float32)
    gb, gc, h = h3[:, :d], h3[:, d:2 * d], h3[:, 2 * d:]
    u = gc * h
    st = st_ref[...]
    pos = lax.broadcasted_iota(jnp.int32, (t, 1), 0) % SUBLANES
    u1 = jnp.where(pos == 0, pltpu.roll(st, t - 1, 0), pltpu.roll(u, 1, 0))
    u2 = jnp.where(pos < 2, st, pltpu.roll(u, 2, 0))
    taps = taps_ref[...]
    conv = taps[0:1, :] * u2 + taps[1:2, :] * u1 + taps[2:3, :] * u
    y = jnp.dot(_bf16(gb * conv), w_out_ref[...], preferred_element_type=jnp.float32)
    o_ref[...] = _layernorm(ALPHA * x + y, g_ref[...], b_ref[...])
    u_ref[...] = u


def _conv_sample(x, st, w_in, taps, w_out, g, b):
    n, d = x.shape
    return pl.pallas_call(
        _conv_sample_body,
        grid=(1,),
        in_specs=[_const_spec(a.shape) for a in (x, st, w_in, taps, w_out, g, b)],
        out_specs=[_const_spec((n, d)), _const_spec((n, d))],
        out_shape=[jax.ShapeDtypeStruct((n, d), jnp.float32)] * 2,
        compiler_params=_params(1),
        name="conv_sample",
    )(x, st, w_in, taps, w_out, g, b)


def _ffn_halves(w_ffn_in, w_ffn_out):
    depth, d, f2 = w_ffn_in.shape
    f = f2 // 2
    fh = f // 2
    assert f % (2 * LANES) == 0
    w = _bf16(w_ffn_in)
    halves = [jnp.concatenate([w[..., h * fh:(h + 1) * fh], w[..., f + h * fh:f + (h + 1) * fh]],
                              axis=-1) for h in range(2)]
    return jnp.stack(halves, axis=1), _bf16(w_ffn_out).reshape(depth, 2, fh, d)


def _ffn_half(x, w_in, w_out):
    fh = w_out.shape[0]
    h = jnp.dot(_bf16(x), w_in, preferred_element_type=jnp.float32)
    gate, up = h[:, :fh], h[:, fh:]
    a = gate * jax.nn.sigmoid(gate) * up
    return jnp.dot(_bf16(a), w_out, preferred_element_type=jnp.float32)


def _ffn_sample_body(x_ref, w_in_ref, w_out_ref, g_ref, b_ref, o_ref):
    x = x_ref[...]
    y = (_ffn_half(x, w_in_ref[0, 0], w_out_ref[0, 0])
         + _ffn_half(x, w_in_ref[0, 1], w_out_ref[0, 1]))
    o_ref[...] = _layernorm(ALPHA * x + y, g_ref[...], b_ref[...])


def _layer_spec(shape, layer):
    nd = len(shape)
    return pl.BlockSpec((1,) + shape[1:], lambda *_: (layer,) + (0,) * (nd - 1),
                        pipeline_mode=pl.Buffered(1))


def _ffn_sample(x, w_in_h, w_out_h, layer, g, b):
    n, d = x.shape
    return pl.pallas_call(
        _ffn_sample_body,
        grid=(1,),
        in_specs=[_const_spec(x.shape), _layer_spec(w_in_h.shape, layer),
                  _layer_spec(w_out_h.shape, layer), _const_spec(g.shape), _const_spec(b.shape)],
        out_specs=_const_spec((n, d)),
        out_shape=jax.ShapeDtypeStruct((n, d), jnp.float32),
        compiler_params=_params(1),
        name="ffn_sample",
    )(x, w_in_h, w_out_h, g, b)


def _rope_tables(pos):
    half = ROT_DIM // 2
    inv = jnp.power(ROPE_THETA, -jnp.arange(0, ROT_DIM, 2, dtype=jnp.float32) / ROT_DIM)
    ang = pos.astype(jnp.float32)[:, None] * inv[None, :]
    cos, sin = jnp.cos(ang), jnp.sin(ang)
    dd = jnp.arange(LANES) % HEAD_DIM
    cos_l = jnp.take(cos, dd % half, axis=1)
    sin_l = jnp.take(sin, dd % half, axis=1)
    c = jnp.where(dd[None, :] < ROT_DIM, cos_l, 1.0)
    s_up = jnp.where(dd[None, :] < half, -sin_l, 0.0)
    s_dn = jnp.where((dd[None, :] >= half) & (dd[None, :] < ROT_DIM), sin_l, 0.0)
    return c, s_up, s_dn


def _rope(x, c, s_up, s_dn):
    half = ROT_DIM // 2
    outs = []
    for g in range(x.shape[1] // LANES):
        xg = x[:, g * LANES:(g + 1) * LANES]
        x_up = pltpu.roll(xg, LANES - half, 1)
        x_dn = pltpu.roll(xg, half, 1)
        outs.append(xg * c + x_up * s_up + x_dn * s_dn)
    return jnp.concatenate(outs, axis=1)


def _qkv_rows(x_ref, w_ref, c_ref, su_ref, sd_ref, rows=slice(None)):
    x = x_ref[rows, :]
    d = x.shape[1]
    h3 = jnp.dot(_bf16(x), w_ref[...], preferred_element_type=jnp.float32)
    c, su, sd = c_ref[rows, :], su_ref[rows, :], sd_ref[rows, :]
    q = _rope(h3[:, :d], c, su, sd)
    k = _rope(h3[:, d:2 * d], c, su, sd)
    return q, k, h3[:, 2 * d:]


def _qkv_sample_body(x_ref, w_ref, c_ref, su_ref, sd_ref, q_ref, k_ref, v_ref):
    q, k, v = _qkv_rows(x_ref, w_ref, c_ref, su_ref, sd_ref)
    q_ref[...] = q * SCALE
    k_ref[...] = k
    v_ref[...] = v


def _qkv_sample(x, w, tables):
    n, d = x.shape
    specs = [_const_spec(a.shape) for a in (x, w) + tuple(tables)]
    return pl.pallas_call(
        _qkv_sample_body,
        grid=(1,),
        in_specs=specs,
        out_specs=[_const_spec((n, d))] * 3,
        out_shape=[jax.ShapeDtypeStruct((n, d), jnp.float32)] * 3,
        compiler_params=_params(1),
        name="qkv_sample",
    )(x, w, *tables)


def _qkv_prompt_body(x_ref, w_ref, c_ref, su_ref, sd_ref,
                     qt_ref, kb_ref, kt_ref, vt_ref, v_ref):
    for rows in _sub_tiles(x_ref.shape[0]):
        q, k, v = _qkv_rows(x_ref, w_ref, c_ref, su_ref, sd_ref, rows)
        qt_ref[0, :, rows] = _bf16((q * (SCALE * LOG2_E)).T)
        kb_ref[rows, :] = _bf16(k)
        kt_ref[0, :, rows] = k.T
        vt_ref[0, :, rows] = _bf16(v.T)
        v_ref[rows, :] = v


def _qkv_prompt(x, w, tables, seq_len):
    n, d = x.shape
    tm = TOKEN_TILE
    tps = seq_len // tm
    tspec = pl.BlockSpec((tm, LANES), lambda i: (i % tps, 0))
    row = pl.BlockSpec((tm, d), lambda i: (i, 0))
    blk = pl.BlockSpec((1, d, tm), lambda i: (i, 0, 0))
    return pl.pallas_call(
        _qkv_prompt_body,
        grid=(n // tm,),
        in_specs=[row, _const_spec(w.shape), tspec, tspec, tspec],
        out_specs=[blk, row, pl.BlockSpec((1, d, tm), lambda i: (i // tps, 0, i % tps)), blk, row],
        out_shape=[
            jax.ShapeDtypeStruct((n // tm, d, tm), jnp.bfloat16),
            jax.ShapeDtypeStruct((n, d), jnp.bfloat16),
            jax.ShapeDtypeStruct((n // seq_len, d, seq_len), jnp.float32),
            jax.ShapeDtypeStruct((n // tm, d, tm), jnp.bfloat16),
            jax.ShapeDtypeStruct((n, d), jnp.float32),
        ],
        compiler_params=_params(1),
        name="qkv_prompt",
    )(x, w, *tables)


def _lambda(lamv_ref):
    lv = lamv_ref[...]
    d1 = jnp.sum(lv[0:1, :] * lv[1:2, :], axis=-1, keepdims=True)
    d2 = jnp.sum(lv[2:3, :] * lv[3:4, :], axis=-1, keepdims=True)
    return jnp.exp(d1) - jnp.exp(d2) + LAM_INIT


def _subln(o, g):
    o = o * lax.rsqrt(jnp.mean(o * o, axis=-1, keepdims=True) + SUBLN_EPS)
    return o * g * (1.0 - LAM_INIT)


def _prompt_attn_body(qlo_ref, qhi_ref, k_ref, vt_ref, lamv_ref, gcol_ref, o_ref,
                      q_ref, s0_ref, s1_ref, bmax_ref, m_ref, acc_ref, *, n_tiles):
    tq, tk = ATTN_Q_TILE, ATTN_KV_TILE
    t_lo = pl.program_id(2)
    t_hi = n_tiles - 1 - t_lo
    neg = jnp.finfo(jnp.float32).min
    feat = lax.broadcasted_iota(jnp.int32, (2 * HEAD_DIM, 1), 0)
    for tile, ref in enumerate((qlo_ref, qhi_ref)):
        q_ref[tile, 0] = jnp.where(feat < HEAD_DIM, ref[0], 0)
        q_ref[tile, 1] = jnp.where(feat >= HEAD_DIM, ref[0], 0)
    ones = jnp.ones((acc_ref.shape[2] - V_DIM, tk), jnp.bfloat16)
    acc_ref[...] = jnp.zeros_like(acc_ref)
    m_ref[...] = jnp.full_like(m_ref, neg)
    s_refs = (s0_ref, s1_ref)

    def block(u):
        if u == 0:
            return 0, t_lo, True
        if u == 1:
            return 1, t_hi, True
        n = u - 2
        return jnp.where(n < t_lo, 0, 1), jnp.where(n < t_lo, n, n - t_lo), False

    def scores(u):
        tile, kv, masked = block(u)
        kblk = k_ref[0, pl.ds(pl.multiple_of(kv * tk, tk), tk), :]
        for c in range(2):
            st = jnp.dot(kblk, q_ref[tile, c], preferred_element_type=jnp.float32)
            if masked:
                key = lax.broadcasted_iota(jnp.int32, (tk, tq), 0)
                qry = lax.broadcasted_iota(jnp.int32, (tk, tq), 1)
                st = jnp.where(key <= qry, st, neg)
            s_refs[u % 2][c] = st
            bmax_ref[u % 2, c] = jnp.max(st, axis=0, keepdims=True)

    def accumulate(u):
        tile, kv, _ = block(u)
        s_ref = s_refs[u % 2]
        vext = jnp.concatenate([vt_ref[0, kv], ones], axis=0)
        for c in range(2):
            m_old = m_ref[tile, c]
            m_new = jnp.maximum(m_old, bmax_ref[u % 2, c])
            alpha = jnp.exp2(m_old - m_new)
            m_ref[tile, c] = m_new
            p = _bf16(jnp.exp2(s_ref[c] - m_new))
            acc_ref[tile, c] = alpha * acc_ref[tile, c] + jnp.dot(
                vext, p, preferred_element_type=jnp.float32)

    n_blocks = n_tiles + 1
    scores(0)
    for u in range(n_blocks):
        if u + 1 < n_blocks:
            scores(u + 1)
        accumulate(u)

    lam = _lambda(lamv_ref)
    for tile in range(2):
        o = (acc_ref[tile, 0, :V_DIM] / acc_ref[tile, 0, V_DIM:V_DIM + 1]
             - lam * (acc_ref[tile, 1, :V_DIM] / acc_ref[tile, 1, V_DIM:V_DIM + 1]))
        o = o * lax.rsqrt(jnp.mean(o * o, axis=0, keepdims=True) + SUBLN_EPS)
        o = o * gcol_ref[...] * (1.0 - LAM_INIT)
        o_ref[0, tile, 0] = o.T.astype(o_ref.dtype)


def _prompt_attn(qt, k, vt, lamv, gcol):
    b, s, d = k.shape
    tq, tk = ATTN_Q_TILE, ATTN_KV_TILE
    nq = s // tq
    assert tq == tk and s % tq == 0 and nq % 2 == 0
    return pl.pallas_call(
        functools.partial(_prompt_attn_body, n_tiles=nq),
        grid=(b, N_HEADS, nq // 2),
        in_specs=[
            pl.BlockSpec((1, V_DIM, tq), lambda bi, j, t: (bi * nq + t, j, 0)),
            pl.BlockSpec((1, V_DIM, tq), lambda bi, j, t: (bi * nq + nq - 1 - t, j, 0)),
            pl.BlockSpec((1, s, V_DIM), lambda bi, j, t: (bi, 0, j)),
            pl.BlockSpec((1, s // tk, V_DIM, tk), lambda bi, j, t: (bi, 0, j, 0)),
            _const_spec(lamv.shape), _const_spec(gcol.shape),
        ],
        out_specs=pl.BlockSpec((1, 2, 1, tq, V_DIM), lambda bi, j, t: (bi, 0, t, 0, j)),
        out_shape=jax.ShapeDtypeStruct((b, 2, nq // 2, tq, d), jnp.bfloat16),
        scratch_shapes=[
            pltpu.VMEM((2, 2, V_DIM, tq), jnp.bfloat16),
            pltpu.VMEM((2, tk, tq), jnp.float32),
            pltpu.VMEM((2, tk, tq), jnp.float32),
            pltpu.VMEM((2, 2, 1, tq), jnp.float32),
            pltpu.VMEM((2, 2, 1, tq), jnp.float32),
            pltpu.VMEM((2, 2, V_DIM + BF16_SUBLANES, tq), jnp.float32),
        ],
        compiler_params=_params(3),
        name="prompt_attn",
    )(qt, qt, k, vt, lamv, gcol)


def _paired_tile_slot(t, nq):
    return jnp.where(t < nq // 2, t, nq // 2 + nq - 1 - t)


class _DecodeRefs(NamedTuple):
    q: object
    kn: object
    vn: object
    lamv: object
    g: object
    k_pages: tuple
    v_pages: tuple
    out: object
    m: object
    l: object
    acc: object
    v3: object
    s: object
    alpha: object


def _decode_math(r):
    ds = r.q.shape[2]
    q3 = _bf16(r.q[0])

    def pair_rows(s):
        return s.reshape(N_HEADS, 2 * ds, s.shape[2])

    def scores(k3):
        return pair_rows(jnp.einsum("hqd,htd->hqt", q3, k3, preferred_element_type=jnp.float32))

    def scores_t(kt3):
        return pair_rows(jnp.einsum("hqd,hdt->hqt", q3, kt3, preferred_element_type=jnp.float32))

    def weighted(p, v3):
        return jnp.einsum("jrt,jte->jre", _bf16(p), v3, preferred_element_type=jnp.float32)

    return ds, scores, scores_t, weighted


def _decode_init(r):
    ds, scores, _, weighted = _decode_math(r)
    page = r.v_pages[0].shape[1]
    kn, vn = r.kn[0], r.vn[0]
    kn3 = _bf16(jnp.concatenate(
        [kn, jnp.zeros((kn.shape[0], page - ds, kn.shape[2]), jnp.float32)], axis=1))
    vn3 = _bf16(jnp.concatenate(
        [vn, jnp.zeros((vn.shape[0], page - ds, vn.shape[2]), jnp.float32)], axis=1))
    s = scores(kn3)
    qi = lax.broadcasted_iota(jnp.int32, s.shape, 1) % ds
    tt = lax.broadcasted_iota(jnp.int32, s.shape, 2)
    s = jnp.where(tt <= qi, s, jnp.finfo(jnp.float32).min)
    m = jnp.max(s, axis=-1, keepdims=True)
    p = jnp.exp(s - m)
    r.m[...] = m
    r.l[...] = jnp.sum(p, axis=-1, keepdims=True)
    r.acc[...] = weighted(p, vn3)


def _decode_values(r):
    page = r.v_pages[0].shape[1]
    for n, ref in enumerate(r.v_pages):
        r.v3[:, n * page:(n + 1) * page, :] = _bf16(pltpu.einshape("tje->jte", ref[0]))


def _decode_scores(r):
    _, _, scores_t, _ = _decode_math(r)
    r.s[...] = scores_t(jnp.concatenate([_bf16(ref[0]) for ref in r.k_pages], axis=2))


def _decode_softmax(r):
    s = r.s[...]
    m_old = r.m[...]
    m_new = jnp.maximum(m_old, jnp.max(s, axis=-1, keepdims=True))
    alpha = jnp.exp(m_old - m_new)
    p = jnp.exp(s - m_new)
    r.l[...] = alpha * r.l[...] + jnp.sum(p, axis=-1, keepdims=True)
    r.m[...] = m_new
    r.alpha[...] = alpha
    r.s[...] = p


def _decode_accumulate(r):
    _, _, _, weighted = _decode_math(r)
    r.acc[...] = r.alpha[...] * r.acc[...] + weighted(r.s[...], r.v3[...])


def _decode_finish(r):
    ds = r.q.shape[2]
    o = r.acc[...] / r.l[...]
    o = o[:, :ds, :] - _lambda(r.lamv) * o[:, ds:, :]
    o = _subln(o, r.g[...])
    for j in range(N_HEADS):
        r.out[0, :, j * V_DIM:(j + 1) * V_DIM] = o[j]


def _ffn_decode_body(pt_ref, x_ref, w_in_ref, w_out_ref, g_ref, b_ref,
                     q_ref, kn_ref, vn_ref, lamv_ref, gsub_ref, *refs, chunks_per_seq):
    pps = PAGES_PER_STEP
    o_ref, od_ref, y_ref = refs[2 * pps:2 * pps + 3]
    dec = _DecodeRefs(q_ref, kn_ref, vn_ref, lamv_ref, gsub_ref,
                      refs[:pps], refs[pps:2 * pps], od_ref, *refs[2 * pps + 3:])
    s = pl.program_id(0)
    half = s % 2
    chunk = s % chunks_per_seq

    @pl.when(chunk == 0)
    def _():
        _decode_init(dec)

    def step(h_static):
        x = x_ref[...]
        _decode_values(dec)
        _decode_scores(dec)
        y = _ffn_half(x, w_in_ref[0, h_static], w_out_ref[0, h_static])
        _decode_softmax(dec)
        _decode_accumulate(dec)
        if h_static == 0:
            y_ref[...] = y
        else:
            o_ref[...] = _layernorm(ALPHA * x + (y_ref[...] + y), g_ref[...], b_ref[...])

    for h_static in range(2):
        pl.when(half == h_static)(functools.partial(step, h_static))

    @pl.when(chunk == chunks_per_seq - 1)
    def _():
        _decode_finish(dec)


def _ffn_decode(x, w_in_h, w_out_h, layer, g, b,
                q3, k_new3, v_new3, cache_kt, cache_v, page_table, lamv, gsub):
    n, d = x.shape
    sb, nh2, ds, hd = q3.shape
    n_pages = page_table.shape[1]
    page = cache_v.shape[1]
    pps = PAGES_PER_STEP
    chunks = n_pages // pps
    rows = SUB_TILE
    n_steps = 2 * (n // rows)
    assert n_pages % pps == 0 and ds == SUBLANES and n_steps == sb * chunks

    def page_spec(shape, pg):
        return pl.BlockSpec(
            (1,) + shape[1:],
            lambda s, pt: (pt[s // chunks * n_pages + s % chunks * pps + pg], 0, 0, 0))

    def seq_spec(shape):
        return pl.BlockSpec((1,) + shape[1:],
                            lambda s, pt: (s // chunks,) + (0,) * (len(shape) - 1))

    row_spec = pl.BlockSpec((rows, d), lambda s, pt: (s // 2, 0))
    grid_spec = pltpu.PrefetchScalarGridSpec(
        num_scalar_prefetch=1,
        grid=(n_steps,),
        in_specs=[row_spec, _layer_spec(w_in_h.shape, layer), _layer_spec(w_out_h.shape, layer),
                  _const_spec(g.shape), _const_spec(b.shape),
                  seq_spec(q3.shape), seq_spec(k_new3.shape), seq_spec(v_new3.shape),
                  _const_spec(lamv.shape), _const_spec(gsub.shape)]
                 + [page_spec(cache_kt.shape, pg) for pg in range(pps)]
                 + [page_spec(cache_v.shape, pg) for pg in range(pps)],
        out_specs=[row_spec, seq_spec((sb, ds, d))],
        scratch_shapes=[
            pltpu.VMEM((rows, d), jnp.float32),
            pltpu.VMEM((N_HEADS, 2 * ds, 1), jnp.float32),
            pltpu.VMEM((N_HEADS, 2 * ds, 1), jnp.float32),
            pltpu.VMEM((N_HEADS, 2 * ds, V_DIM), jnp.float32),
            pltpu.VMEM((N_HEADS, pps * page, V_DIM), jnp.bfloat16),
            pltpu.VMEM((N_HEADS, 2 * ds, pps * page), jnp.float32),
            pltpu.VMEM((N_HEADS, 2 * ds, 1), jnp.float32),
        ],
    )
    return pl.pallas_call(
        functools.partial(_ffn_decode_body, chunks_per_seq=chunks),
        grid_spec=grid_spec,
        out_shape=[jax.ShapeDtypeStruct((n, d), jnp.float32),
                   jax.ShapeDtypeStruct((sb, ds, d), jnp.float32)],
        compiler_params=_params(1),
        name="ffn_decode",
    )(page_table.reshape(-1), x, w_in_h, w_out_h, g, b, q3, k_new3, v_new3, lamv, gsub,
      *([cache_kt] * pps), *([cache_v] * pps))


def _proj_ln_body(o_ref, x_ref, w_ref, g_ref, b_ref, y_ref):
    for rows in _sub_tiles(x_ref.shape[0]):
        y = jnp.dot(_bf16(o_ref[rows, :]), w_ref[...], preferred_element_type=jnp.float32)
        y_ref[rows, :] = _layernorm(ALPHA * x_ref[rows, :] + y, g_ref[...], b_ref[...])


def _proj_ln(o, x, w, g, b, o_tile=lambda i: i):
    n, d = x.shape
    tm = min(TOKEN_TILE, n)
    row = pl.BlockSpec((tm, d), lambda i: (i, 0))
    return pl.pallas_call(
        _proj_ln_body,
        grid=(n // tm,),
        in_specs=[pl.BlockSpec((tm, d), lambda i: (o_tile(i), 0)),
                  row, _const_spec(w.shape), _const_spec(g.shape), _const_spec(b.shape)],
        out_specs=row,
        out_shape=jax.ShapeDtypeStruct((n, d), jnp.float32),
        compiler_params=_params(1),
        name="proj_ln",
    )(o, x, w, g, b)


def kernel(x_prompt, x_sample, state_conv, cache_k, cache_v, page_table, w_conv_in, w_conv, w_conv_out, w_qkv, lambda_q1, lambda_k1, lambda_q2, lambda_k2, subln_g, w_attn_out, ln_mix_g, ln_mix_b, w_ffn_in, w_ffn_out, ln_ffn_g, ln_ffn_b):
    b, s, d = x_prompt.shape
    db, ds, _ = x_sample.shape
    n_pool, page = cache_k.shape[:2]
    past_len = page_table.shape[1] * page
    f32 = jnp.float32
    assert ds == SUBLANES and s % TOKEN_TILE == 0

    w_conv_in_b, w_conv_out_b = _bf16(w_conv_in), _bf16(w_conv_out)
    w_qkv_b, w_attn_out_b = _bf16(w_qkv), _bf16(w_attn_out)
    w_ffn_in_h, w_ffn_out_h = _ffn_halves(w_ffn_in, w_ffn_out)
    taps = w_conv.astype(f32)
    row = lambda a: a.reshape(1, -1).astype(f32)
    lamv = jnp.stack([lambda_q1, lambda_k1, lambda_q2, lambda_k2]).astype(f32)
    g_sub = row(subln_g)

    xp = x_prompt.reshape(b * s, d)
    xs = x_sample.reshape(db * ds, d)

    i = 0
    xp, tail_p = _conv_prompt(xp, w_conv_in_b, taps, w_conv_out_b,
                              row(ln_mix_g[i]), row(ln_mix_b[i]), s)
    conv_p = tail_p.reshape(b, SUBLANES, d)[:, SUBLANES - (CONV_WIDTH - 1):]
    st = jnp.pad(state_conv, ((0, 0), (0, ds - (CONV_WIDTH - 1)), (0, 0))).reshape(db * ds, d)
    xs, u_s = _conv_sample(xs, st, w_conv_in_b, taps, w_conv_out_b,
                           row(ln_mix_g[i]), row(ln_mix_b[i]))
    conv_s = u_s.reshape(db, ds, d)[:, ds - (CONV_WIDTH - 1):]
    xs = _ffn_sample(xs, w_ffn_in_h, w_ffn_out_h, i, row(ln_ffn_g[i]), row(ln_ffn_b[i]))

    tab_s = _rope_tables(past_len + jnp.arange(db * ds) % ds)
    qs, k_s, v_s = _qkv_sample(xs, w_qkv_b, tab_s)
    heads_first = lambda a, nh: a.reshape(db, ds, nh, d // nh).transpose(0, 2, 1, 3)
    q3, k_new3, v_new3 = (heads_first(qs, 2 * N_HEADS), heads_first(k_s, 2 * N_HEADS),
                          heads_first(v_s, N_HEADS))
    cache_kt = cache_k.transpose(0, 2, 3, 1)
    hb = db // 2

    def ffn_decode(x, layer, seqs):
        return _ffn_decode(x, w_ffn_in_h, w_ffn_out_h, layer,
                           row(ln_ffn_g[layer]), row(ln_ffn_b[layer]),
                           q3[seqs], k_new3[seqs], v_new3[seqs], cache_kt, cache_v,
                           page_table[seqs], lamv, g_sub)

    xp, os_lo = ffn_decode(xp, i, slice(0, hb))

    i = 1
    tab_p = _rope_tables(jnp.arange(s))
    qt_p, kb_p, kt_p, vt_p, v_p = _qkv_prompt(xp, w_qkv_b, tab_p, s)
    k_p = kt_p.reshape(b, 2 * N_HEADS, HEAD_DIM, s).transpose(0, 3, 1, 2)

    op = _prompt_attn(qt_p, kb_p.reshape(b, s, d),
                      vt_p.reshape(b, s // ATTN_KV_TILE, d, ATTN_KV_TILE),
                      lamv, g_sub.reshape(-1, 1))
    nq = s // ATTN_Q_TILE
    assert ATTN_Q_TILE == TOKEN_TILE
    xp = _proj_ln(op.reshape(b * s, d), xp, w_attn_out_b, row(ln_mix_g[i]), row(ln_mix_b[i]),
                  o_tile=lambda r: r // nq * nq + _paired_tile_slot(r % nq, nq))
    xp, os_hi = ffn_decode(xp, i, slice(hb, db))
    os_ = jnp.concatenate([os_lo, os_hi], axis=0)
    xs = _proj_ln(os_.reshape(db * ds, d), xs, w_attn_out_b, row(ln_mix_g[i]), row(ln_mix_b[i]))
    xs = _ffn_sample(xs, w_ffn_in_h, w_ffn_out_h, i, row(ln_ffn_g[i]), row(ln_ffn_b[i]))

    return (xp.reshape(b, s, d), xs.reshape(db, ds, d), conv_p,
            k_p, v_p.reshape(b, s, N_HEADS, V_DIM),
            conv_s,
            k_s.reshape(db, ds, 2 * N_HEADS, HEAD_DIM), v_s.reshape(db, ds, N_HEADS, V_DIM))
```

```python
import functools
import math
from typing import NamedTuple

import jax
import jax.numpy as jnp
from jax import lax
from jax.experimental import pallas as pl
from jax.experimental.pallas import tpu as pltpu

N_HEADS = 8
HEAD_DIM = 64
V_DIM = 2 * HEAD_DIM
ROT_DIM = HEAD_DIM // 4
ROPE_THETA = 500000.0
CONV_WIDTH = 3
DEPTH = 2
LN_EPS = 1e-5
SUBLN_EPS = 1e-5
ALPHA = (2 * DEPTH) ** 0.25
SCALE = HEAD_DIM ** -0.5
LOG2_E = math.log2(math.e)
ATTN_LAYER = 1
LAM_INIT = 0.8 - 0.6 * math.exp(-0.3 * ATTN_LAYER)

LANES = 128
SUBLANES = 8
BF16_SUBLANES = 16
MXU_DIM = 256
VMEM_LIMIT_BYTES = 56 * 1024 * 1024

TOKEN_TILE = 512
SUB_TILE = 256
ATTN_Q_TILE = 512
ATTN_KV_TILE = 512
PAGES_PER_STEP = 8

_NT = (((1,), (1,)), ((), ()))


def _bf16(x):
    return x.astype(jnp.bfloat16)


def _layernorm(y, g, b):
    mu = jnp.mean(y, axis=-1, keepdims=True)
    yc = y - mu
    var = jnp.mean(yc * yc, axis=-1, keepdims=True)
    return yc * lax.rsqrt(var + LN_EPS) * g + b


def _const_spec(shape):
    nd = len(shape)
    return pl.BlockSpec(shape, lambda *_: (0,) * nd, pipeline_mode=pl.Buffered(1))


def _params(n_axes, flags=None):
    return pltpu.CompilerParams(
        dimension_semantics=("arbitrary",) * n_axes,
        vmem_limit_bytes=VMEM_LIMIT_BYTES,
        flags=flags)


def _sub_tiles(rows):
    step = min(rows, SUB_TILE)
    return [slice(r, r + step) for r in range(0, rows, step)]


def _conv_prompt_body(x_ref, w_in_ref, taps_ref, w_out_ref, g_ref, b_ref,
                      o_ref, tail_ref, carry_ref, *, tiles_per_seq):
    i = pl.program_id(0)
    d = x_ref.shape[1]

    @pl.when(i % tiles_per_seq == 0)
    def _():
        carry_ref[...] = jnp.zeros_like(carry_ref)

    x = x_ref[...]
    t = x.shape[0]
    h3 = jnp.dot(_bf16(x), w_in_ref[...], preferred_element_type=jnp.float32)
    gb, gc, h = h3[:, :d], h3[:, d:2 * d], h3[:, 2 * d:]
    u = gc * h
    row = lax.broadcasted_iota(jnp.int32, (t, 1), 0)
    c6 = carry_ref[SUBLANES - 2:SUBLANES - 1, :]
    c7 = carry_ref[SUBLANES - 1:SUBLANES, :]
    u1 = jnp.where(row == 0, c7, pltpu.roll(u, 1, 0))
    u2 = jnp.where(row == 0, c6, jnp.where(row == 1, c7, pltpu.roll(u, 2, 0)))
    taps = taps_ref[...]
    conv = taps[0:1, :] * u2 + taps[1:2, :] * u1 + taps[2:3, :] * u
    y = jnp.dot(_bf16(gb * conv), w_out_ref[...], preferred_element_type=jnp.float32)
    o_ref[...] = _layernorm(ALPHA * x + y, g_ref[...], b_ref[...])

    carry_ref[...] = u[t - SUBLANES:, :]

    @pl.when(i % tiles_per_seq == tiles_per_seq - 1)
    def _():
        tail_ref[...] = u[t - SUBLANES:, :]


def _conv_prompt(x, w_in, taps, w_out, g, b, seq_len):
    n, d = x.shape
    tm = TOKEN_TILE
    tiles_per_seq = seq_len // tm
    return pl.pallas_call(
        functools.partial(_conv_prompt_body, tiles_per_seq=tiles_per_seq),
        grid=(n // tm,),
        in_specs=[
            pl.BlockSpec((tm, d), lambda i: (i, 0)),
            _const_spec(w_in.shape), _const_spec(taps.shape), _const_spec(w_out.shape),
            _const_spec(g.shape), _const_spec(b.shape),
        ],
        out_specs=[
            pl.BlockSpec((tm, d), lambda i: (i, 0)),
            pl.BlockSpec((SUBLANES, d), lambda i: (i // tiles_per_seq, 0)),
        ],
        out_shape=[
            jax.ShapeDtypeStruct((n, d), jnp.float32),
            jax.ShapeDtypeStruct((n // seq_len * SUBLANES, d), jnp.float32),
        ],
        scratch_shapes=[pltpu.VMEM((SUBLANES, d), jnp.float32)],
        compiler_params=_params(1),
        name="conv_prompt",
    )(x, w_in, taps, w_out, g, b)


def _conv_sample_body(x_ref, st_ref, w_in_ref, taps_ref, w_out_ref, g_ref, b_ref,
                      o_ref, u_ref):
    x = x_ref[...]
    t, d = x.shape
    h3 = jnp.dot(_bf16(x), w_in_ref[...], preferred_element_type=jnp.float32)
    gb, gc, h = h3[:, :d], h3[:, d:2 * d], h3[:, 2 * d:]
    u = gc * h
    st = st_ref[...]
    pos = lax.broadcasted_iota(jnp.int32, (t, 1), 0) % SUBLANES
    u1 = jnp.where(pos == 0, pltpu.roll(st, t - 1, 0), pltpu.roll(u, 1, 0))
    u2 = jnp.where(pos < 2, st, pltpu.roll(u, 2, 0))
    taps = taps_ref[...]
    conv = taps[0:1, :] * u2 + taps[1:2, :] * u1 + taps[2:3, :] * u
    y = jnp.dot(_bf16(gb * conv), w_out_ref[...], preferred_element_type=jnp.float32)
    o_ref[...] = _layernorm(ALPHA * x + y, g_ref[...], b_ref[...])
    u_ref[...] = u


def _conv_sample(x, st, w_in, taps, w_out, g, b):
    n, d = x.shape
    return pl.pallas_call(
        _conv_sample_body,
        grid=(1,),
        in_specs=[_const_spec(a.shape) for a in (x, st, w_in, taps, w_out, g, b)],
        out_specs=[_const_spec((n, d)), _const_spec((n, d))],
        out_shape=[jax.ShapeDtypeStruct((n, d), jnp.float32)] * 2,
        compiler_params=_params(1),
        name="conv_sample",
    )(x, st, w_in, taps, w_out, g, b)


def _ffn_halves(w_ffn_in, w_ffn_out):
    depth, d, f2 = w_ffn_in.shape
    f = f2 // 2
    fh = f // 2
    assert f % (2 * LANES) == 0
    w = w_ffn_in
    halves = [jnp.concatenate([w[..., h * fh:(h + 1) * fh], w[..., f + h * fh:f + (h + 1) * fh]],
                              axis=-1) for h in range(2)]
    return _bf16(jnp.stack(halves, axis=1)), _bf16(w_ffn_out).reshape(depth, 2, fh, d)


def _ffn_half(x, w_in, w_out):
    fh = w_out.shape[0]
    h = jnp.dot(_bf16(x), w_in, preferred_element_type=jnp.float32)
    gate, up = h[:, :fh], h[:, fh:]
    a = gate * jax.nn.sigmoid(gate) * up
    return jnp.dot(_bf16(a), w_out, preferred_element_type=jnp.float32)


def _ffn_sample_body(x_ref, w_in_ref, w_out_ref, g_ref, b_ref, o_ref):
    x = x_ref[...]
    y = (_ffn_half(x, w_in_ref[0, 0], w_out_ref[0, 0])
         + _ffn_half(x, w_in_ref[0, 1], w_out_ref[0, 1]))
    o_ref[...] = _layernorm(ALPHA * x + y, g_ref[...], b_ref[...])


def _layer_spec(shape, layer):
    nd = len(shape)
    return pl.BlockSpec((1,) + shape[1:], lambda *_: (layer,) + (0,) * (nd - 1),
                        pipeline_mode=pl.Buffered(1))


def _ffn_sample(x, w_in_h, w_out_h, layer, g, b):
    n, d = x.shape
    return pl.pallas_call(
        _ffn_sample_body,
        grid=(1,),
        in_specs=[_const_spec(x.shape), _layer_spec(w_in_h.shape, layer),
                  _layer_spec(w_out_h.shape, layer), _const_spec(g.shape), _const_spec(b.shape)],
        out_specs=_const_spec((n, d)),
        out_shape=jax.ShapeDtypeStruct((n, d), jnp.float32),
        compiler_params=_params(1),
        name="ffn_sample",
    )(x, w_in_h, w_out_h, g, b)


def _rope_tables(pos):
    half = ROT_DIM // 2
    inv = jnp.power(ROPE_THETA, -jnp.arange(0, ROT_DIM, 2, dtype=jnp.float32) / ROT_DIM)
    ang = pos.astype(jnp.float32)[:, None] * inv[None, :]
    cos, sin = jnp.cos(ang), jnp.sin(ang)
    dd = jnp.arange(LANES) % HEAD_DIM
    cos_l = jnp.take(cos, dd % half, axis=1)
    sin_l = jnp.take(sin, dd % half, axis=1)
    c = jnp.where(dd[None, :] < ROT_DIM, cos_l, 1.0)
    s_up = jnp.where(dd[None, :] < half, -sin_l, 0.0)
    s_dn = jnp.where((dd[None, :] >= half) & (dd[None, :] < ROT_DIM), sin_l, 0.0)
    return c, s_up, s_dn


def _rope(x, c, s_up, s_dn):
    half = ROT_DIM // 2
    outs = []
    for g in range(x.shape[1] // LANES):
        xg = x[:, g * LANES:(g + 1) * LANES]
        x_up = pltpu.roll(xg, LANES - half, 1)
        x_dn = pltpu.roll(xg, half, 1)
        outs.append(xg * c + x_up * s_up + x_dn * s_dn)
    return jnp.concatenate(outs, axis=1)


def _qkv_rows(x_ref, w_ref, c_ref, su_ref, sd_ref, rows=slice(None)):
    x = x_ref[rows, :]
    d = x.shape[1]
    h3 = jnp.dot(_bf16(x), w_ref[...], preferred_element_type=jnp.float32)
    c, su, sd = c_ref[rows, :], su_ref[rows, :], sd_ref[rows, :]
    q = _rope(h3[:, :d], c, su, sd)
    k = _rope(h3[:, d:2 * d], c, su, sd)
    return q, k, h3[:, 2 * d:]


def _qkv_sample_body(x_ref, w_ref, c_ref, su_ref, sd_ref, q_ref, k_ref, v_ref):
    q, k, v = _qkv_rows(x_ref, w_ref, c_ref, su_ref, sd_ref)
    q_ref[...] = q * SCALE
    k_ref[...] = k
    v_ref[...] = v


def _qkv_sample(x, w, tables):
    n, d = x.shape
    specs = [_const_spec(a.shape) for a in (x, w) + tuple(tables)]
    return pl.pallas_call(
        _qkv_sample_body,
        grid=(1,),
        in_specs=specs,
        out_specs=[_const_spec((n, d))] * 3,
        out_shape=[jax.ShapeDtypeStruct((n, d), jnp.float32)] * 3,
        compiler_params=_params(1),
        name="qkv_sample",
    )(x, w, *tables)


def _qkv_prompt_body(x_ref, w_ref, c_ref, su_ref, sd_ref,
                     qt_ref, kb_ref, kt_ref, vt_ref, v_ref):
    for rows in _sub_tiles(x_ref.shape[0]):
        q, k, v = _qkv_rows(x_ref, w_ref, c_ref, su_ref, sd_ref, rows)
        qt_ref[0, :, rows] = _bf16((q * (SCALE * LOG2_E)).T)
        kb_ref[rows, :] = _bf16(k)
        kt_ref[0, :, rows] = k.T
        vt_ref[0, :, rows] = _bf16(v.T)
        v_ref[rows, :] = v


def _qkv_prompt(x, w, tables, seq_len):
    n, d = x.shape
    tm = TOKEN_TILE
    tps = seq_len // tm
    tspec = pl.BlockSpec((tm, LANES), lambda i: (i % tps, 0))
    row = pl.BlockSpec((tm, d), lambda i: (i, 0))
    blk = pl.BlockSpec((1, d, tm), lambda i: (i, 0, 0))
    return pl.pallas_call(
        _qkv_prompt_body,
        grid=(n // tm,),
        in_specs=[row, _const_spec(w.shape), tspec, tspec, tspec],
        out_specs=[blk, row, pl.BlockSpec((1, d, tm), lambda i: (i // tps, 0, i % tps)), blk, row],
        out_shape=[
            jax.ShapeDtypeStruct((n // tm, d, tm), jnp.bfloat16),
            jax.ShapeDtypeStruct((n, d), jnp.bfloat16),
            jax.ShapeDtypeStruct((n // seq_len, d, seq_len), jnp.float32),
            jax.ShapeDtypeStruct((n // tm, d, tm), jnp.bfloat16),
            jax.ShapeDtypeStruct((n, d), jnp.float32),
        ],
        compiler_params=_params(1),
        name="qkv_prompt",
    )(x, w, *tables)


def _lambda(lamv_ref):
    lv = lamv_ref[...]
    d1 = jnp.sum(lv[0:1, :] * lv[1:2, :], axis=-1, keepdims=True)
    d2 = jnp.sum(lv[2:3, :] * lv[3:4, :], axis=-1, keepdims=True)
    return jnp.exp(d1) - jnp.exp(d2) + LAM_INIT


def _subln(o, g):
    o = o * lax.rsqrt(jnp.mean(o * o, axis=-1, keepdims=True) + SUBLN_EPS)
    return o * g * (1.0 - LAM_INIT)


def _prompt_attn_body(qlo_ref, qhi_ref, k_ref, vt_ref, lamv_ref, gcol_ref, o_ref,
                      q_ref, s0_ref, s1_ref, bmax_ref, m_ref, acc_ref, *, n_tiles):
    tq, tk = ATTN_Q_TILE, ATTN_KV_TILE
    t_lo = pl.program_id(2)
    t_hi = n_tiles - 1 - t_lo
    neg = jnp.finfo(jnp.float32).min
    feat = lax.broadcasted_iota(jnp.int32, (2 * HEAD_DIM, 1), 0)
    for tile, ref in enumerate((qlo_ref, qhi_ref)):
        q_ref[tile, 0] = jnp.where(feat < HEAD_DIM, ref[0], 0)
        q_ref[tile, 1] = jnp.where(feat >= HEAD_DIM, ref[0], 0)
    ones = jnp.ones((acc_ref.shape[2] - V_DIM, tk), jnp.bfloat16)
    acc_ref[...] = jnp.zeros_like(acc_ref)
    m_ref[...] = jnp.full_like(m_ref, neg)
    s_refs = (s0_ref, s1_ref)

    def block(u):
        if u == 0:
            return 0, t_lo, True
        if u == 1:
            return 1, t_hi, True
        n = u - 2
        return jnp.where(n < t_lo, 0, 1), jnp.where(n < t_lo, n, n - t_lo), False

    def scores(u):
        tile, kv, masked = block(u)
        kblk = k_ref[0, pl.ds(pl.multiple_of(kv * tk, tk), tk), :]
        for c in range(2):
            st = jnp.dot(kblk, q_ref[tile, c], preferred_element_type=jnp.float32)
            if masked:
                key = lax.broadcasted_iota(jnp.int32, (tk, tq), 0)
                qry = lax.broadcasted_iota(jnp.int32, (tk, tq), 1)
                st = jnp.where(key <= qry, st, neg)
            s_refs[u % 2][c] = st
            bmax_ref[u % 2, c] = jnp.max(st, axis=0, keepdims=True)

    def accumulate(u):
        tile, kv, _ = block(u)
        s_ref = s_refs[u % 2]
        vext = jnp.concatenate([vt_ref[0, kv], ones], axis=0)
        for c in range(2):
            m_old = m_ref[tile, c]
            m_new = jnp.maximum(m_old, bmax_ref[u % 2, c])
            alpha = jnp.exp2(m_old - m_new)
            m_ref[tile, c] = m_new
            p = _bf16(jnp.exp2(s_ref[c] - m_new))
            acc_ref[tile, c] = alpha * acc_ref[tile, c] + jnp.dot(
                vext, p, preferred_element_type=jnp.float32)

    n_blocks = n_tiles + 1
    scores(0)
    for u in range(n_blocks):
        if u + 1 < n_blocks:
            scores(u + 1)
        accumulate(u)

    lam = _lambda(lamv_ref)
    for tile in range(2):
        o = (acc_ref[tile, 0, :V_DIM] / acc_ref[tile, 0, V_DIM:V_DIM + 1]
             - lam * (acc_ref[tile, 1, :V_DIM] / acc_ref[tile, 1, V_DIM:V_DIM + 1]))
        o = o * lax.rsqrt(jnp.mean(o * o, axis=0, keepdims=True) + SUBLN_EPS)
        o = o * gcol_ref[...] * (1.0 - LAM_INIT)
        o_ref[0, tile, 0] = o.T.astype(o_ref.dtype)


def _prompt_attn(qt, k, vt, lamv, gcol):
    b, s, d = k.shape
    tq, tk = ATTN_Q_TILE, ATTN_KV_TILE
    nq = s // tq
    assert tq == tk and s % tq == 0 and nq % 2 == 0
    return pl.pallas_call(
        functools.partial(_prompt_attn_body, n_tiles=nq),
        grid=(b, N_HEADS, nq // 2),
        in_specs=[
            pl.BlockSpec((1, V_DIM, tq), lambda bi, j, t: (bi * nq + t, j, 0)),
            pl.BlockSpec((1, V_DIM, tq), lambda bi, j, t: (bi * nq + nq - 1 - t, j, 0)),
            pl.BlockSpec((1, s, V_DIM), lambda bi, j, t: (bi, 0, j)),
            pl.BlockSpec((1, s // tk, V_DIM, tk), lambda bi, j, t: (bi, 0, j, 0)),
            _const_spec(lamv.shape), _const_spec(gcol.shape),
        ],
        out_specs=pl.BlockSpec((1, 2, 1, tq, V_DIM), lambda bi, j, t: (bi, 0, t, 0, j)),
        out_shape=jax.ShapeDtypeStruct((b, 2, nq // 2, tq, d), jnp.bfloat16),
        scratch_shapes=[
            pltpu.VMEM((2, 2, V_DIM, tq), jnp.bfloat16),
            pltpu.VMEM((2, tk, tq), jnp.float32),
            pltpu.VMEM((2, tk, tq), jnp.float32),
            pltpu.VMEM((2, 2, 1, tq), jnp.float32),
            pltpu.VMEM((2, 2, 1, tq), jnp.float32),
            pltpu.VMEM((2, 2, V_DIM + BF16_SUBLANES, tq), jnp.float32),
        ],
        compiler_params=_params(3),
        name="prompt_attn",
    )(qt, qt, k, vt, lamv, gcol)


def _paired_tile_slot(t, nq):
    return jnp.where(t < nq // 2, t, nq // 2 + nq - 1 - t)


class _DecodeRefs(NamedTuple):
    q: object
    kn: object
    vn: object
    lamv: object
    g: object
    k_pages: tuple
    v_pages: tuple
    out: object
    m: object
    l: object
    acc: object
    v3: object
    s: object
    alpha: object


def _decode_math(r):
    ds = r.q.shape[2]
    q3 = _bf16(r.q[0])

    def pair_rows(s):
        return s.reshape(N_HEADS, 2 * ds, s.shape[2])

    def scores(k3):
        return pair_rows(jnp.einsum("hqd,htd->hqt", q3, k3, preferred_element_type=jnp.float32))

    def scores_t(kt3):
        return pair_rows(jnp.einsum("hqd,hdt->hqt", q3, kt3, preferred_element_type=jnp.float32))

    def weighted(p, v3):
        return jnp.einsum("jrt,jte->jre", _bf16(p), v3, preferred_element_type=jnp.float32)

    return ds, scores, scores_t, weighted


def _decode_init(r):
    ds, scores, _, weighted = _decode_math(r)
    page = r.v_pages[0].shape[1]
    kn, vn = r.kn[0], r.vn[0]
    kn3 = _bf16(jnp.concatenate(
        [kn, jnp.zeros((kn.shape[0], page - ds, kn.shape[2]), jnp.float32)], axis=1))
    vn3 = _bf16(jnp.concatenate(
        [vn, jnp.zeros((vn.shape[0], page - ds, vn.shape[2]), jnp.float32)], axis=1))
    s = scores(kn3)
    qi = lax.broadcasted_iota(jnp.int32, s.shape, 1) % ds
    tt = lax.broadcasted_iota(jnp.int32, s.shape, 2)
    s = jnp.where(tt <= qi, s, jnp.finfo(jnp.float32).min)
    m = jnp.max(s, axis=-1, keepdims=True)
    p = jnp.exp(s - m)
    r.m[...] = m
    r.l[...] = jnp.sum(p, axis=-1, keepdims=True)
    r.acc[...] = weighted(p, vn3)


def _head_rows(tokens):
    return tokens + SUBLANES


def _decode_values(r):
    page = r.v_pages[0].shape[1]
    stride = _head_rows(page * len(r.v_pages))
    for n, ref in enumerate(r.v_pages):
        for t in range(page):
            r.v3[pl.ds(n * page + t, N_HEADS, stride=stride), :] = ref[0, t]


def _decode_scores(r):
    _, _, scores_t, _ = _decode_math(r)
    r.s[...] = scores_t(jnp.concatenate([_bf16(ref[0]) for ref in r.k_pages], axis=2))


def _decode_softmax(r):
    s = r.s[...]
    m_old = r.m[...]
    m_new = jnp.maximum(m_old, jnp.max(s, axis=-1, keepdims=True))
    alpha = jnp.exp(m_old - m_new)
    p = jnp.exp(s - m_new)
    r.l[...] = alpha * r.l[...] + jnp.sum(p, axis=-1, keepdims=True)
    r.m[...] = m_new
    r.alpha[...] = alpha
    r.s[...] = p


def _decode_accumulate(r):
    _, _, _, weighted = _decode_math(r)
    tokens = r.s.shape[2]
    stride = _head_rows(tokens)
    v3 = jnp.stack([_bf16(r.v3[j * stride:j * stride + tokens, :]) for j in range(N_HEADS)])
    r.acc[...] = r.alpha[...] * r.acc[...] + weighted(r.s[...], v3)


def _decode_finish(r):
    ds = r.q.shape[2]
    o = r.acc[...] / r.l[...]
    o = o[:, :ds, :] - _lambda(r.lamv) * o[:, ds:, :]
    o = _subln(o, r.g[...])
    for j in range(N_HEADS):
        r.out[0, :, j * V_DIM:(j + 1) * V_DIM] = o[j]


def _ffn_decode_body(pt_ref, x_ref, w_in_ref, w_out_ref, g_ref, b_ref,
                     q_ref, kn_ref, vn_ref, lamv_ref, gsub_ref, *refs, chunks_per_seq):
    pps = PAGES_PER_STEP
    o_ref, od_ref, y_ref = refs[2 * pps:2 * pps + 3]
    dec = _DecodeRefs(q_ref, kn_ref, vn_ref, lamv_ref, gsub_ref,
                      refs[:pps], refs[pps:2 * pps], od_ref, *refs[2 * pps + 3:])
    s = pl.program_id(0)
    half = s % 2
    chunk = s % chunks_per_seq

    @pl.when(chunk == 0)
    def _():
        _decode_init(dec)

    def step(h_static):
        x = x_ref[...]
        _decode_values(dec)
        _decode_scores(dec)
        y = _ffn_half(x, w_in_ref[0, h_static], w_out_ref[0, h_static])
        _decode_softmax(dec)
        _decode_accumulate(dec)
        if h_static == 0:
            y_ref[...] = y
        else:
            o_ref[...] = _layernorm(ALPHA * x + (y_ref[...] + y), g_ref[...], b_ref[...])

    for h_static in range(2):
        pl.when(half == h_static)(functools.partial(step, h_static))

    @pl.when(chunk == chunks_per_seq - 1)
    def _():
        _decode_finish(dec)


def _ffn_decode(x, w_in_h, w_out_h, layer, g, b,
                q3, k_new3, v_new3, cache_kt, cache_v, page_table, lamv, gsub):
    n, d = x.shape
    sb, nh2, ds, hd = q3.shape
    n_pages = page_table.shape[1]
    page = cache_v.shape[1]
    pps = PAGES_PER_STEP
    chunks = n_pages // pps
    rows = SUB_TILE
    n_steps = 2 * (n // rows)
    assert n_pages % pps == 0 and ds == SUBLANES and n_steps == sb * chunks

    def page_spec(shape, pg):
        return pl.BlockSpec(
            (1,) + shape[1:],
            lambda s, pt: (pt[s // chunks * n_pages + s % chunks * pps + pg], 0, 0, 0))

    def seq_spec(shape):
        return pl.BlockSpec((1,) + shape[1:],
                            lambda s, pt: (s // chunks,) + (0,) * (len(shape) - 1))

    row_spec = pl.BlockSpec((rows, d), lambda s, pt: (s // 2, 0))
    grid_spec = pltpu.PrefetchScalarGridSpec(
        num_scalar_prefetch=1,
        grid=(n_steps,),
        in_specs=[row_spec, _layer_spec(w_in_h.shape, layer), _layer_spec(w_out_h.shape, layer),
                  _const_spec(g.shape), _const_spec(b.shape),
                  seq_spec(q3.shape), seq_spec(k_new3.shape), seq_spec(v_new3.shape),
                  _const_spec(lamv.shape), _const_spec(gsub.shape)]
                 + [page_spec(cache_kt.shape, pg) for pg in range(pps)]
                 + [page_spec(cache_v.shape, pg) for pg in range(pps)],
        out_specs=[row_spec, seq_spec((sb, ds, d))],
        scratch_shapes=[
            pltpu.VMEM((rows, d), jnp.float32),
            pltpu.VMEM((N_HEADS, 2 * ds, 1), jnp.float32),
            pltpu.VMEM((N_HEADS, 2 * ds, 1), jnp.float32),
            pltpu.VMEM((N_HEADS, 2 * ds, V_DIM), jnp.float32),
            pltpu.VMEM((N_HEADS * _head_rows(pps * page), V_DIM), jnp.float32),
            pltpu.VMEM((N_HEADS, 2 * ds, pps * page), jnp.float32),
            pltpu.VMEM((N_HEADS, 2 * ds, 1), jnp.float32),
        ],
    )
    return pl.pallas_call(
        functools.partial(_ffn_decode_body, chunks_per_seq=chunks),
        grid_spec=grid_spec,
        out_shape=[jax.ShapeDtypeStruct((n, d), jnp.float32),
                   jax.ShapeDtypeStruct((sb, ds, d), jnp.float32)],
        compiler_params=_params(1),
        name="ffn_decode",
    )(page_table.reshape(-1), x, w_in_h, w_out_h, g, b, q3, k_new3, v_new3, lamv, gsub,
      *([cache_kt] * pps), *([cache_v] * pps))


def _proj_ln_body(o_ref, x_ref, w_ref, g_ref, b_ref, y_ref):
    for rows in _sub_tiles(x_ref.shape[0]):
        y = jnp.dot(_bf16(o_ref[rows, :]), w_ref[...], preferred_element_type=jnp.float32)
        y_ref[rows, :] = _layernorm(ALPHA * x_ref[rows, :] + y, g_ref[...], b_ref[...])


def _proj_ln(o, x, w, g, b, o_tile=lambda i: i):
    n, d = x.shape
    tm = min(TOKEN_TILE, n)
    row = pl.BlockSpec((tm, d), lambda i: (i, 0))
    return pl.pallas_call(
        _proj_ln_body,
        grid=(n // tm,),
        in_specs=[pl.BlockSpec((tm, d), lambda i: (o_tile(i), 0)),
                  row, _const_spec(w.shape), _const_spec(g.shape), _const_spec(b.shape)],
        out_specs=row,
        out_shape=jax.ShapeDtypeStruct((n, d), jnp.float32),
        compiler_params=_params(1),
        name="proj_ln",
    )(o, x, w, g, b)


def kernel(x_prompt, x_sample, state_conv, cache_k, cache_v, page_table, w_conv_in, w_conv, w_conv_out, w_qkv, lambda_q1, lambda_k1, lambda_q2, lambda_k2, subln_g, w_attn_out, ln_mix_g, ln_mix_b, w_ffn_in, w_ffn_out, ln_ffn_g, ln_ffn_b):
    b, s, d = x_prompt.shape
    db, ds, _ = x_sample.shape
    n_pool, page = cache_k.shape[:2]
    past_len = page_table.shape[1] * page
    f32 = jnp.float32
    assert ds == SUBLANES and s % TOKEN_TILE == 0

    w_conv_in_b, w_conv_out_b = _bf16(w_conv_in), _bf16(w_conv_out)
    w_qkv_b, w_attn_out_b = _bf16(w_qkv), _bf16(w_attn_out)
    w_ffn_in_h, w_ffn_out_h = _ffn_halves(w_ffn_in, w_ffn_out)
    taps = w_conv.astype(f32)
    row = lambda a: a.reshape(1, -1).astype(f32)
    lamv = jnp.stack([lambda_q1, lambda_k1, lambda_q2, lambda_k2]).astype(f32)
    g_sub = row(subln_g)

    xp = x_prompt.reshape(b * s, d)
    xs = x_sample.reshape(db * ds, d)

    i = 0
    xp, tail_p = _conv_prompt(xp, w_conv_in_b, taps, w_conv_out_b,
                              row(ln_mix_g[i]), row(ln_mix_b[i]), s)
    conv_p = tail_p.reshape(b, SUBLANES, d)[:, SUBLANES - (CONV_WIDTH - 1):]
    st = jnp.pad(state_conv, ((0, 0), (0, ds - (CONV_WIDTH - 1)), (0, 0))).reshape(db * ds, d)
    xs, u_s = _conv_sample(xs, st, w_conv_in_b, taps, w_conv_out_b,
                           row(ln_mix_g[i]), row(ln_mix_b[i]))
    conv_s = u_s.reshape(db, ds, d)[:, ds - (CONV_WIDTH - 1):]
    xs = _ffn_sample(xs, w_ffn_in_h, w_ffn_out_h, i, row(ln_ffn_g[i]), row(ln_ffn_b[i]))

    tab_s = _rope_tables(past_len + jnp.arange(db * ds) % ds)
    qs, k_s, v_s = _qkv_sample(xs, w_qkv_b, tab_s)
    heads_first = lambda a, nh: a.reshape(db, ds, nh, d // nh).transpose(0, 2, 1, 3)
    q3, k_new3, v_new3 = (heads_first(qs, 2 * N_HEADS), heads_first(k_s, 2 * N_HEADS),
                          heads_first(v_s, N_HEADS))
    cache_kt = cache_k.transpose(0, 2, 3, 1)
    hb = db // 2

    def ffn_decode(x, layer, seqs):
        return _ffn_decode(x, w_ffn_in_h, w_ffn_out_h, layer,
                           row(ln_ffn_g[layer]), row(ln_ffn_b[layer]),
                           q3[seqs], k_new3[seqs], v_new3[seqs], cache_kt, cache_v,
                           page_table[seqs], lamv, g_sub)

    xp, os_lo = ffn_decode(xp, i, slice(0, hb))

    i = 1
    tab_p = _rope_tables(jnp.arange(s))
    qt_p, kb_p, kt_p, vt_p, v_p = _qkv_prompt(xp, w_qkv_b, tab_p, s)
    k_p = kt_p.reshape(b, 2 * N_HEADS, HEAD_DIM, s).transpose(0, 3, 1, 2)

    op = _prompt_attn(qt_p, kb_p.reshape(b, s, d),
                      vt_p.reshape(b, s // ATTN_KV_TILE, d, ATTN_KV_TILE),
                      lamv, g_sub.reshape(-1, 1))
    nq = s // ATTN_Q_TILE
    assert ATTN_Q_TILE == TOKEN_TILE
    xp = _proj_ln(op.reshape(b * s, d), xp, w_attn_out_b, row(ln_mix_g[i]), row(ln_mix_b[i]),
                  o_tile=lambda r: r // nq * nq + _paired_tile_slot(r % nq, nq))
    xp, os_hi = ffn_decode(xp, i, slice(hb, db))
    os_ = jnp.concatenate([os_lo, os_hi], axis=0)
    xs = _proj_ln(os_.reshape(db * ds, d), xs, w_attn_out_b, row(ln_mix_g[i]), row(ln_mix_b[i]))
    xs = _ffn_sample(xs, w_ffn_in_h, w_ffn_out_h, i, row(ln_ffn_g[i]), row(ln_ffn_b[i]))

    return (xp.reshape(b, s, d), xs.reshape(db, ds, d), conv_p,
            k_p, v_p.reshape(b, s, N_HEADS, V_DIM),
            conv_s,
            k_s.reshape(db, ds, 2 * N_HEADS, HEAD_DIM), v_s.reshape(db, ds, N_HEADS, V_DIM))
```

```python
import functools
import math
from typing import NamedTuple

import jax
import jax.numpy as jnp
from jax import lax
from jax.experimental import pallas as pl
from jax.experimental.pallas import tpu as pltpu

N_HEADS = 8
HEAD_DIM = 64
V_DIM = 2 * HEAD_DIM
ROT_DIM = HEAD_DIM // 4
ROPE_THETA = 500000.0
CONV_WIDTH = 3
DEPTH = 2
LN_EPS = 1e-5
SUBLN_EPS = 1e-5
ALPHA = (2 * DEPTH) ** 0.25
SCALE = HEAD_DIM ** -0.5
LOG2_E = math.log2(math.e)
ATTN_LAYER = 1
LAM_INIT = 0.8 - 0.6 * math.exp(-0.3 * ATTN_LAYER)

LANES = 128
SUBLANES = 8
BF16_SUBLANES = 16
MXU_DIM = 256
VMEM_LIMIT_BYTES = 56 * 1024 * 1024

TOKEN_TILE = 1024
SUB_TILE = 256
ATTN_Q_TILE = 512
ATTN_KV_TILE = 512
PAGES_PER_STEP = 8

_NT = (((1,), (1,)), ((), ()))


def _bf16(x):
    return x.astype(jnp.bfloat16)


def _layernorm(y, g, b):
    mu = jnp.mean(y, axis=-1, keepdims=True)
    yc = y - mu
    var = jnp.mean(yc * yc, axis=-1, keepdims=True)
    return yc * lax.rsqrt(var + LN_EPS) * g + b


def _const_spec(shape):
    nd = len(shape)
    return pl.BlockSpec(shape, lambda *_: (0,) * nd, pipeline_mode=pl.Buffered(1))


def _params(n_axes, flags=None):
    return pltpu.CompilerParams(
        dimension_semantics=("arbitrary",) * n_axes,
        vmem_limit_bytes=VMEM_LIMIT_BYTES,
        flags=flags)


def _sub_tiles(rows):
    step = min(rows, SUB_TILE)
    return [slice(r, r + step) for r in range(0, rows, step)]


def _conv_prompt_body(x_ref, w_in_ref, taps_ref, w_out_ref, g_ref, b_ref,
                      o_ref, tail_ref, carry_ref, *, tiles_per_seq):
    i = pl.program_id(0)
    d = x_ref.shape[1]

    @pl.when(i % tiles_per_seq == 0)
    def _():
        carry_ref[...] = jnp.zeros_like(carry_ref)

    x = x_ref[...]
    t = x.shape[0]
    h3 = jnp.dot(_bf16(x), w_in_ref[...], preferred_element_type=jnp.float32)
    gb, gc, h = h3[:, :d], h3[:, d:2 * d], h3[:, 2 * d:]
    u = gc * h
    row = lax.broadcasted_iota(jnp.int32, (t, 1), 0)
    c6 = carry_ref[SUBLANES - 2:SUBLANES - 1, :]
    c7 = carry_ref[SUBLANES - 1:SUBLANES, :]
    u1 = jnp.where(row == 0, c7, pltpu.roll(u, 1, 0))
    u2 = jnp.where(row == 0, c6, jnp.where(row == 1, c7, pltpu.roll(u, 2, 0)))
    taps = taps_ref[...]
    conv = taps[0:1, :] * u2 + taps[1:2, :] * u1 + taps[2:3, :] * u
    y = jnp.dot(_bf16(gb * conv), w_out_ref[...], preferred_element_type=jnp.float32)
    o_ref[...] = _layernorm(ALPHA * x + y, g_ref[...], b_ref[...])

    carry_ref[...] = u[t - SUBLANES:, :]

    @pl.when(i % tiles_per_seq == tiles_per_seq - 1)
    def _():
        tail_ref[...] = u[t - SUBLANES:, :]


def _conv_prompt(x, w_in, taps, w_out, g, b, seq_len):
    n, d = x.shape
    tm = TOKEN_TILE
    tiles_per_seq = seq_len // tm
    return pl.pallas_call(
        functools.partial(_conv_prompt_body, tiles_per_seq=tiles_per_seq),
        grid=(n // tm,),
        in_specs=[
            pl.BlockSpec((tm, d), lambda i: (i, 0)),
            _const_spec(w_in.shape), _const_spec(taps.shape), _const_spec(w_out.shape),
            _const_spec(g.shape), _const_spec(b.shape),
        ],
        out_specs=[
            pl.BlockSpec((tm, d), lambda i: (i, 0)),
            pl.BlockSpec((SUBLANES, d), lambda i: (i // tiles_per_seq, 0)),
        ],
        out_shape=[
            jax.ShapeDtypeStruct((n, d), jnp.float32),
            jax.ShapeDtypeStruct((n // seq_len * SUBLANES, d), jnp.float32),
        ],
        scratch_shapes=[pltpu.VMEM((SUBLANES, d), jnp.float32)],
        compiler_params=_params(1),
        name="conv_prompt",
    )(x, w_in, taps, w_out, g, b)


def _conv_sample_body(x_ref, st_ref, w_in_ref, taps_ref, w_out_ref, g_ref, b_ref,
                      o_ref, u_ref):
    x = x_ref[...]
    t, d = x.shape
    h3 = jnp.dot(_bf16(x), w_in_ref[...], preferred_element_type=jnp.float32)
    gb, gc, h = h3[:, :d], h3[:, d:2 * d], h3[:, 2 * d:]
    u = gc * h
    st = st_ref[...]
    pos = lax.broadcasted_iota(jnp.int32, (t, 1), 0) % SUBLANES
    u1 = jnp.where(pos == 0, pltpu.roll(st, t - 1, 0), pltpu.roll(u, 1, 0))
    u2 = jnp.where(pos < 2, st, pltpu.roll(u, 2, 0))
    taps = taps_ref[...]
    conv = taps[0:1, :] * u2 + taps[1:2, :] * u1 + taps[2:3, :] * u
    y = jnp.dot(_bf16(gb * conv), w_out_ref[...], preferred_element_type=jnp.float32)
    o_ref[...] = _layernorm(ALPHA * x + y, g_ref[...], b_ref[...])
    u_ref[...] = u


def _conv_sample(x, st, w_in, taps, w_out, g, b):
    n, d = x.shape
    return pl.pallas_call(
        _conv_sample_body,
        grid=(1,),
        in_specs=[_const_spec(a.shape) for a in (x, st, w_in, taps, w_out, g, b)],
        out_specs=[_const_spec((n, d)), _const_spec((n, d))],
        out_shape=[jax.ShapeDtypeStruct((n, d), jnp.float32)] * 2,
        compiler_params=_params(1),
        name="conv_sample",
    )(x, st, w_in, taps, w_out, g, b)


def _ffn_split(f):
    assert f % MXU_DIM == 0
    cut = (f // MXU_DIM + 1) // 2 * MXU_DIM
    return (0, cut), (cut, f)


def _ffn_part(x, w_in_ref, w_out_ref, lo, hi):
    f = w_out_ref.shape[1]
    xb = _bf16(x)
    gate = jnp.dot(xb, w_in_ref[0, :, lo:hi], preferred_element_type=jnp.float32)
    up = jnp.dot(xb, w_in_ref[0, :, f + lo:f + hi], preferred_element_type=jnp.float32)
    a = gate * jax.nn.sigmoid(gate) * up
    return jnp.dot(_bf16(a), w_out_ref[0, lo:hi, :], preferred_element_type=jnp.float32)


def _ffn_sample_body(x_ref, w_in_ref, w_out_ref, g_ref, b_ref, o_ref):
    x = x_ref[...]
    y = _ffn_part(x, w_in_ref, w_out_ref, 0, w_out_ref.shape[1])
    o_ref[...] = _layernorm(ALPHA * x + y, g_ref[...], b_ref[...])


def _layer_spec(shape, layer):
    nd = len(shape)
    return pl.BlockSpec((1,) + shape[1:], lambda *_: (layer,) + (0,) * (nd - 1),
                        pipeline_mode=pl.Buffered(1))


def _ffn_sample(x, w_in, w_out, layer, g, b):
    n, d = x.shape
    return pl.pallas_call(
        _ffn_sample_body,
        grid=(1,),
        in_specs=[_const_spec(x.shape), _layer_spec(w_in.shape, layer),
                  _layer_spec(w_out.shape, layer), _const_spec(g.shape), _const_spec(b.shape)],
        out_specs=_const_spec((n, d)),
        out_shape=jax.ShapeDtypeStruct((n, d), jnp.float32),
        compiler_params=_params(1),
        name="ffn_sample",
    )(x, w_in, w_out, g, b)


def _rope_tables(pos):
    half = ROT_DIM // 2
    inv = jnp.power(ROPE_THETA, -jnp.arange(0, ROT_DIM, 2, dtype=jnp.float32) / ROT_DIM)
    ang = pos.astype(jnp.float32)[:, None] * inv[None, :]
    cos, sin = jnp.cos(ang), jnp.sin(ang)
    dd = jnp.arange(LANES) % HEAD_DIM
    cos_l = jnp.take(cos, dd % half, axis=1)
    sin_l = jnp.take(sin, dd % half, axis=1)
    c = jnp.where(dd[None, :] < ROT_DIM, cos_l, 1.0)
    s_up = jnp.where(dd[None, :] < half, -sin_l, 0.0)
    s_dn = jnp.where((dd[None, :] >= half) & (dd[None, :] < ROT_DIM), sin_l, 0.0)
    return c, s_up, s_dn


def _rope(x, c, s_up, s_dn):
    half = ROT_DIM // 2
    outs = []
    for g in range(x.shape[1] // LANES):
        xg = x[:, g * LANES:(g + 1) * LANES]
        x_up = pltpu.roll(xg, LANES - half, 1)
        x_dn = pltpu.roll(xg, half, 1)
        outs.append(xg * c + x_up * s_up + x_dn * s_dn)
    return jnp.concatenate(outs, axis=1)


def _qkv_rows(x_ref, w_ref, c_ref, su_ref, sd_ref, rows=slice(None)):
    x = x_ref[rows, :]
    d = x.shape[1]
    h3 = jnp.dot(_bf16(x), w_ref[...], preferred_element_type=jnp.float32)
    c, su, sd = c_ref[rows, :], su_ref[rows, :], sd_ref[rows, :]
    q = _rope(h3[:, :d], c, su, sd)
    k = _rope(h3[:, d:2 * d], c, su, sd)
    return q, k, h3[:, 2 * d:]


def _qkv_sample_body(x_ref, w_ref, c_ref, su_ref, sd_ref, q_ref, k_ref, v_ref):
    q, k, v = _qkv_rows(x_ref, w_ref, c_ref, su_ref, sd_ref)
    q_ref[...] = q * SCALE
    k_ref[...] = k
    v_ref[...] = v


def _qkv_sample(x, w, tables):
    n, d = x.shape
    specs = [_const_spec(a.shape) for a in (x, w) + tuple(tables)]
    return pl.pallas_call(
        _qkv_sample_body,
        grid=(1,),
        in_specs=specs,
        out_specs=[_const_spec((n, d))] * 3,
        out_shape=[jax.ShapeDtypeStruct((n, d), jnp.float32)] * 3,
        compiler_params=_params(1),
        name="qkv_sample",
    )(x, w, *tables)


def _qkv_prompt_body(x_ref, w_ref, c_ref, su_ref, sd_ref,
                     qt_ref, kb_ref, kt_ref, vt_ref, v_ref):
    ta = qt_ref.shape[2]
    for c in range(qt_ref.shape[0]):
        for sub in _sub_tiles(ta):
            rows = slice(c * ta + sub.start, c * ta + sub.stop)
            q, k, v = _qkv_rows(x_ref, w_ref, c_ref, su_ref, sd_ref, rows)
            qt_ref[c, :, sub] = _bf16((q * (SCALE * LOG2_E)).T)
            kb_ref[rows, :] = _bf16(k)
            kt_ref[0, :, rows] = k.T
            vt_ref[c, :, sub] = _bf16(v.T)
            v_ref[rows, :] = v


def _qkv_prompt(x, w, tables, seq_len):
    n, d = x.shape
    tm = TOKEN_TILE
    ta = ATTN_Q_TILE
    assert ATTN_Q_TILE == ATTN_KV_TILE and tm % ta == 0
    tps = seq_len // tm
    tspec = pl.BlockSpec((tm, LANES), lambda i: (i % tps, 0))
    row = pl.BlockSpec((tm, d), lambda i: (i, 0))
    blk = pl.BlockSpec((tm // ta, d, ta), lambda i: (i, 0, 0))
    return pl.pallas_call(
        _qkv_prompt_body,
        grid=(n // tm,),
        in_specs=[row, _const_spec(w.shape), tspec, tspec, tspec],
        out_specs=[blk, row, pl.BlockSpec((1, d, tm), lambda i: (i // tps, 0, i % tps)), blk, row],
        out_shape=[
            jax.ShapeDtypeStruct((n // ta, d, ta), jnp.bfloat16),
            jax.ShapeDtypeStruct((n, d), jnp.bfloat16),
            jax.ShapeDtypeStruct((n // seq_len, d, seq_len), jnp.float32),
            jax.ShapeDtypeStruct((n // ta, d, ta), jnp.bfloat16),
            jax.ShapeDtypeStruct((n, d), jnp.float32),
        ],
        compiler_params=_params(1),
        name="qkv_prompt",
    )(x, w, *tables)


def _lambda(lamv_ref):
    lv = lamv_ref[...]
    d1 = jnp.sum(lv[0:1, :] * lv[1:2, :], axis=-1, keepdims=True)
    d2 = jnp.sum(lv[2:3, :] * lv[3:4, :], axis=-1, keepdims=True)
    return jnp.exp(d1) - jnp.exp(d2) + LAM_INIT


def _subln(o, g):
    o = o * lax.rsqrt(jnp.mean(o * o, axis=-1, keepdims=True) + SUBLN_EPS)
    return o * g * (1.0 - LAM_INIT)


def _prompt_attn_body(qlo_ref, qhi_ref, k_ref, vt_ref, lamv_ref, gcol_ref, o_ref,
                      q_ref, s0_ref, s1_ref, bmax_ref, m_ref, acc_ref, *, n_tiles):
    tq, tk = ATTN_Q_TILE, ATTN_KV_TILE
    t_lo = pl.program_id(2)
    t_hi = n_tiles - 1 - t_lo
    neg = jnp.finfo(jnp.float32).min
    feat = lax.broadcasted_iota(jnp.int32, (2 * HEAD_DIM, 1), 0)
    for tile, ref in enumerate((qlo_ref, qhi_ref)):
        q_ref[tile, 0] = jnp.where(feat < HEAD_DIM, ref[0], 0)
        q_ref[tile, 1] = jnp.where(feat >= HEAD_DIM, ref[0], 0)
    ones = jnp.ones((acc_ref.shape[2] - V_DIM, tk), jnp.bfloat16)
    acc_ref[...] = jnp.zeros_like(acc_ref)
    m_ref[...] = jnp.full_like(m_ref, neg)
    s_refs = (s0_ref, s1_ref)

    def block(u):
        if u == 0:
            return 0, t_lo, True
        if u == 1:
            return 1, t_hi, True
        n = u - 2
        return jnp.where(n < t_lo, 0, 1), jnp.where(n < t_lo, n, n - t_lo), False

    def scores(u):
        tile, kv, masked = block(u)
        kblk = k_ref[0, pl.ds(pl.multiple_of(kv * tk, tk), tk), :]
        for c in range(2):
            st = jnp.dot(kblk, q_ref[tile, c], preferred_element_type=jnp.float32)
            if masked:
                key = lax.broadcasted_iota(jnp.int32, (tk, tq), 0)
                qry = lax.broadcasted_iota(jnp.int32, (tk, tq), 1)
                st = jnp.where(key <= qry, st, neg)
            s_refs[u % 2][c] = st
            bmax_ref[u % 2, c] = jnp.max(st, axis=0, keepdims=True)

    def accumulate(u):
        tile, kv, _ = block(u)
        s_ref = s_refs[u % 2]
        vext = jnp.concatenate([vt_ref[0, kv], ones], axis=0)
        for c in range(2):
            m_old = m_ref[tile, c]
            m_new = jnp.maximum(m_old, bmax_ref[u % 2, c])
            alpha = jnp.exp2(m_old - m_new)
            m_ref[tile, c] = m_new
            p = _bf16(jnp.exp2(s_ref[c] - m_new))
            acc_ref[tile, c] = alpha * acc_ref[tile, c] + jnp.dot(
                vext, p, preferred_element_type=jnp.float32)

    n_blocks = n_tiles + 1
    scores(0)
    for u in range(n_blocks):
        if u + 1 < n_blocks:
            scores(u + 1)
        accumulate(u)

    lam = _lambda(lamv_ref)
    for tile in range(2):
        o = (acc_ref[tile, 0, :V_DIM] / acc_ref[tile, 0, V_DIM:V_DIM + 1]
             - lam * (acc_ref[tile, 1, :V_DIM] / acc_ref[tile, 1, V_DIM:V_DIM + 1]))
        o = o * lax.rsqrt(jnp.mean(o * o, axis=0, keepdims=True) + SUBLN_EPS)
        o = o * gcol_ref[...] * (1.0 - LAM_INIT)
        o_ref[0, tile, 0] = o.T.astype(o_ref.dtype)


def _prompt_attn(qt, k, vt, lamv, gcol):
    b, s, d = k.shape
    tq, tk = ATTN_Q_TILE, ATTN_KV_TILE
    nq = s // tq
    assert tq == tk and s % tq == 0 and nq % 2 == 0
    return pl.pallas_call(
        functools.partial(_prompt_attn_body, n_tiles=nq),
        grid=(b, N_HEADS, nq // 2),
        in_specs=[
            pl.BlockSpec((1, V_DIM, tq), lambda bi, j, t: (bi * nq + t, j, 0)),
            pl.BlockSpec((1, V_DIM, tq), lambda bi, j, t: (bi * nq + nq - 1 - t, j, 0)),
            pl.BlockSpec((1, s, V_DIM), lambda bi, j, t: (bi, 0, j)),
            pl.BlockSpec((1, s // tk, V_DIM, tk), lambda bi, j, t: (bi, 0, j, 0)),
            _const_spec(lamv.shape), _const_spec(gcol.shape),
        ],
        out_specs=pl.BlockSpec((1, 2, 1, tq, V_DIM), lambda bi, j, t: (bi, 0, t, 0, j)),
        out_shape=jax.ShapeDtypeStruct((b, 2, nq // 2, tq, d), jnp.bfloat16),
        scratch_shapes=[
            pltpu.VMEM((2, 2, V_DIM, tq), jnp.bfloat16),
            pltpu.VMEM((2, tk, tq), jnp.float32),
            pltpu.VMEM((2, tk, tq), jnp.float32),
            pltpu.VMEM((2, 2, 1, tq), jnp.float32),
            pltpu.VMEM((2, 2, 1, tq), jnp.float32),
            pltpu.VMEM((2, 2, V_DIM + BF16_SUBLANES, tq), jnp.float32),
        ],
        compiler_params=_params(3),
        name="prompt_attn",
    )(qt, qt, k, vt, lamv, gcol)


def _paired_tile_slot(t, nq):
    return jnp.where(t < nq // 2, t, nq // 2 + nq - 1 - t)


class _DecodeRefs(NamedTuple):
    q: object
    kn: object
    vn: object
    lamv: object
    g: object
    k_pages: tuple
    v_pages: tuple
    out: object
    m: object
    l: object
    acc: object
    v3: object
    s: object
    alpha: object


def _decode_math(r):
    ds = r.q.shape[2]
    q3 = _bf16(r.q[0])

    def pair_rows(s):
        return s.reshape(N_HEADS, 2 * ds, s.shape[2])

    def scores(k3):
        return pair_rows(jnp.einsum("hqd,htd->hqt", q3, k3, preferred_element_type=jnp.float32))

    def scores_t(kt3):
        return pair_rows(jnp.einsum("hqd,hdt->hqt", q3, kt3, preferred_element_type=jnp.float32))

    def weighted(p, v3):
        return jnp.einsum("jrt,jte->jre", _bf16(p), v3, preferred_element_type=jnp.float32)

    return ds, scores, scores_t, weighted


def _decode_init(r):
    ds, scores, _, weighted = _decode_math(r)
    page = r.v_pages[0].shape[1]
    kn, vn = r.kn[0], r.vn[0]
    kn3 = _bf16(jnp.concatenate(
        [kn, jnp.zeros((kn.shape[0], page - ds, kn.shape[2]), jnp.float32)], axis=1))
    vn3 = _bf16(jnp.concatenate(
        [vn, jnp.zeros((vn.shape[0], page - ds, vn.shape[2]), jnp.float32)], axis=1))
    s = scores(kn3)
    qi = lax.broadcasted_iota(jnp.int32, s.shape, 1) % ds
    tt = lax.broadcasted_iota(jnp.int32, s.shape, 2)
    s = jnp.where(tt <= qi, s, jnp.finfo(jnp.float32).min)
    m = jnp.max(s, axis=-1, keepdims=True)
    p = jnp.exp(s - m)
    r.m[...] = m
    r.l[...] = jnp.sum(p, axis=-1, keepdims=True)
    r.acc[...] = weighted(p, vn3)


def _head_rows(tokens):
    return tokens + SUBLANES


def _decode_values(r):
    page = r.v_pages[0].shape[1]
    stride = _head_rows(page * len(r.v_pages))
    for n, ref in enumerate(r.v_pages):
        for t in range(page):
            r.v3[pl.ds(n * page + t, N_HEADS, stride=stride), :] = ref[0, t]


def _decode_scores(r):
    _, _, scores_t, _ = _decode_math(r)
    r.s[...] = scores_t(jnp.concatenate([_bf16(ref[0]) for ref in r.k_pages], axis=2))


def _decode_softmax(r):
    s = r.s[...]
    m_old = r.m[...]
    m_new = jnp.maximum(m_old, jnp.max(s, axis=-1, keepdims=True))
    alpha = jnp.exp(m_old - m_new)
    p = jnp.exp(s - m_new)
    r.l[...] = alpha * r.l[...] + jnp.sum(p, axis=-1, keepdims=True)
    r.m[...] = m_new
    r.alpha[...] = alpha
    r.s[...] = p


def _decode_accumulate(r):
    _, _, _, weighted = _decode_math(r)
    tokens = r.s.shape[2]
    stride = _head_rows(tokens)
    v3 = jnp.stack([_bf16(r.v3[j * stride:j * stride + tokens, :]) for j in range(N_HEADS)])
    r.acc[...] = r.alpha[...] * r.acc[...] + weighted(r.s[...], v3)


def _decode_finish(r):
    ds = r.q.shape[2]
    o = r.acc[...] / r.l[...]
    o = o[:, :ds, :] - _lambda(r.lamv) * o[:, ds:, :]
    o = _subln(o, r.g[...])
    for j in range(N_HEADS):
        r.out[0, :, j * V_DIM:(j + 1) * V_DIM] = o[j]


def _ffn_decode_body(pt_ref, x_ref, w_in_ref, w_out_ref, g_ref, b_ref,
                     q_ref, kn_ref, vn_ref, lamv_ref, gsub_ref, *refs, chunks_per_seq):
    pps = PAGES_PER_STEP
    o_ref, od_ref, y_ref = refs[2 * pps:2 * pps + 3]
    dec = _DecodeRefs(q_ref, kn_ref, vn_ref, lamv_ref, gsub_ref,
                      refs[:pps], refs[pps:2 * pps], od_ref, *refs[2 * pps + 3:])
    s = pl.program_id(0)
    half = s % 2
    chunk = s % chunks_per_seq

    @pl.when(chunk == 0)
    def _():
        _decode_init(dec)

    def step(h_static):
        x = x_ref[...]
        _decode_values(dec)
        _decode_scores(dec)
        y = _ffn_part(x, w_in_ref, w_out_ref, *_ffn_split(w_out_ref.shape[1])[h_static])
        _decode_softmax(dec)
        _decode_accumulate(dec)
        if h_static == 0:
            y_ref[...] = y
        else:
            o_ref[...] = _layernorm(ALPHA * x + (y_ref[...] + y), g_ref[...], b_ref[...])

    for h_static in range(2):
        pl.when(half == h_static)(functools.partial(step, h_static))

    @pl.when(chunk == chunks_per_seq - 1)
    def _():
        _decode_finish(dec)


def _ffn_decode(x, w_in, w_out, layer, g, b,
                q3, k_new3, v_new3, cache_kt, cache_v, page_table, lamv, gsub):
    n, d = x.shape
    sb, nh2, ds, hd = q3.shape
    n_pages = page_table.shape[1]
    page = cache_v.shape[1]
    pps = PAGES_PER_STEP
    chunks = n_pages // pps
    rows = SUB_TILE
    n_steps = 2 * (n // rows)
    assert n_pages % pps == 0 and ds == SUBLANES and n_steps == sb * chunks

    def page_spec(shape, pg):
        return pl.BlockSpec(
            (1,) + shape[1:],
            lambda s, pt: (pt[s // chunks * n_pages + s % chunks * pps + pg], 0, 0, 0))

    def seq_spec(shape):
        return pl.BlockSpec((1,) + shape[1:],
                            lambda s, pt: (s // chunks,) + (0,) * (len(shape) - 1))

    row_spec = pl.BlockSpec((rows, d), lambda s, pt: (s // 2, 0))
    grid_spec = pltpu.PrefetchScalarGridSpec(
        num_scalar_prefetch=1,
        grid=(n_steps,),
        in_specs=[row_spec, _layer_spec(w_in.shape, layer), _layer_spec(w_out.shape, layer),
                  _const_spec(g.shape), _const_spec(b.shape),
                  seq_spec(q3.shape), seq_spec(k_new3.shape), seq_spec(v_new3.shape),
                  _const_spec(lamv.shape), _const_spec(gsub.shape)]
                 + [page_spec(cache_kt.shape, pg) for pg in range(pps)]
                 + [page_spec(cache_v.shape, pg) for pg in range(pps)],
        out_specs=[row_spec, seq_spec((sb, ds, d))],
        scratch_shapes=[
            pltpu.VMEM((rows, d), jnp.float32),
            pltpu.VMEM((N_HEADS, 2 * ds, 1), jnp.float32),
            pltpu.VMEM((N_HEADS, 2 * ds, 1), jnp.float32),
            pltpu.VMEM((N_HEADS, 2 * ds, V_DIM), jnp.float32),
            pltpu.VMEM((N_HEADS * _head_rows(pps * page), V_DIM), jnp.float32),
            pltpu.VMEM((N_HEADS, 2 * ds, pps * page), jnp.float32),
            pltpu.VMEM((N_HEADS, 2 * ds, 1), jnp.float32),
        ],
    )
    return pl.pallas_call(
        functools.partial(_ffn_decode_body, chunks_per_seq=chunks),
        grid_spec=grid_spec,
        out_shape=[jax.ShapeDtypeStruct((n, d), jnp.float32),
                   jax.ShapeDtypeStruct((sb, ds, d), jnp.float32)],
        compiler_params=_params(1),
        name="ffn_decode",
    )(page_table.reshape(-1), x, w_in, w_out, g, b, q3, k_new3, v_new3, lamv, gsub,
      *([cache_kt] * pps), *([cache_v] * pps))


def _proj_ln_body(*refs):
    o_refs, (x_ref, w_ref, g_ref, b_ref, y_ref) = refs[:-5], refs[-5:]
    to = o_refs[0].shape[0]
    for n, o_ref in enumerate(o_refs):
        for sub in _sub_tiles(to):
            rows = slice(n * to + sub.start, n * to + sub.stop)
            y = jnp.dot(_bf16(o_ref[sub, :]), w_ref[...], preferred_element_type=jnp.float32)
            y_ref[rows, :] = _layernorm(ALPHA * x_ref[rows, :] + y, g_ref[...], b_ref[...])


def _proj_ln(o, x, w, g, b, tm, to, o_tile=lambda i: i):
    n, d = x.shape
    assert n % tm == 0 and tm % to == 0
    per = tm // to
    row = pl.BlockSpec((tm, d), lambda i: (i, 0))
    o_specs = [pl.BlockSpec((to, d), lambda i, k=k: (o_tile(i * per + k), 0)) for k in range(per)]
    return pl.pallas_call(
        _proj_ln_body,
        grid=(n // tm,),
        in_specs=o_specs + [row, _const_spec(w.shape), _const_spec(g.shape), _const_spec(b.shape)],
        out_specs=row,
        out_shape=jax.ShapeDtypeStruct((n, d), jnp.float32),
        compiler_params=_params(1),
        name="proj_ln",
    )(*([o] * per), x, w, g, b)


def kernel(x_prompt, x_sample, state_conv, cache_k, cache_v, page_table, w_conv_in, w_conv, w_conv_out, w_qkv, lambda_q1, lambda_k1, lambda_q2, lambda_k2, subln_g, w_attn_out, ln_mix_g, ln_mix_b, w_ffn_in, w_ffn_out, ln_ffn_g, ln_ffn_b):
    b, s, d = x_prompt.shape
    db, ds, _ = x_sample.shape
    n_pool, page = cache_k.shape[:2]
    past_len = page_table.shape[1] * page
    f32 = jnp.float32
    assert ds == SUBLANES and s % TOKEN_TILE == 0

    w_conv_in_b, w_conv_out_b = _bf16(w_conv_in), _bf16(w_conv_out)
    w_qkv_b, w_attn_out_b = _bf16(w_qkv), _bf16(w_attn_out)
    w_ffn_in_b, w_ffn_out_b = _bf16(w_ffn_in), _bf16(w_ffn_out)
    taps = w_conv.astype(f32)
    row = lambda a: a.reshape(1, -1).astype(f32)
    lamv = jnp.stack([lambda_q1, lambda_k1, lambda_q2, lambda_k2]).astype(f32)
    g_sub = row(subln_g)

    xp = x_prompt.reshape(b * s, d)
    xs = x_sample.reshape(db * ds, d)

    i = 0
    xp, tail_p = _conv_prompt(xp, w_conv_in_b, taps, w_conv_out_b,
                              row(ln_mix_g[i]), row(ln_mix_b[i]), s)
    conv_p = tail_p.reshape(b, SUBLANES, d)[:, SUBLANES - (CONV_WIDTH - 1):]
    st = jnp.pad(state_conv, ((0, 0), (0, ds - (CONV_WIDTH - 1)), (0, 0))).reshape(db * ds, d)
    xs, u_s = _conv_sample(xs, st, w_conv_in_b, taps, w_conv_out_b,
                           row(ln_mix_g[i]), row(ln_mix_b[i]))
    conv_s = u_s.reshape(db, ds, d)[:, ds - (CONV_WIDTH - 1):]
    xs = _ffn_sample(xs, w_ffn_in_b, w_ffn_out_b, i, row(ln_ffn_g[i]), row(ln_ffn_b[i]))

    tab_s = _rope_tables(past_len + jnp.arange(db * ds) % ds)
    qs, k_s, v_s = _qkv_sample(xs, w_qkv_b, tab_s)
    heads_first = lambda a, nh: a.reshape(db, ds, nh, d // nh).transpose(0, 2, 1, 3)
    q3, k_new3, v_new3 = (heads_first(qs, 2 * N_HEADS), heads_first(k_s, 2 * N_HEADS),
                          heads_first(v_s, N_HEADS))
    cache_kt = cache_k.transpose(0, 2, 3, 1)
    hb = db // 2

    def ffn_decode(x, layer, seqs):
        return _ffn_decode(x, w_ffn_in_b, w_ffn_out_b, layer,
                           row(ln_ffn_g[layer]), row(ln_ffn_b[layer]),
                           q3[seqs], k_new3[seqs], v_new3[seqs], cache_kt, cache_v,
                           page_table[seqs], lamv, g_sub)

    xp, os_lo = ffn_decode(xp, i, slice(0, hb))

    i = 1
    tab_p = _rope_tables(jnp.arange(s))
    qt_p, kb_p, kt_p, vt_p, v_p = _qkv_prompt(xp, w_qkv_b, tab_p, s)
    k_p = kt_p.reshape(b, 2 * N_HEADS, HEAD_DIM, s).transpose(0, 3, 1, 2)

    op = _prompt_attn(qt_p, kb_p.reshape(b, s, d),
                      vt_p.reshape(b, s // ATTN_KV_TILE, d, ATTN_KV_TILE),
                      lamv, g_sub.reshape(-1, 1))
    nq = s // ATTN_Q_TILE
    xp = _proj_ln(op.reshape(b * s, d), xp, w_attn_out_b, row(ln_mix_g[i]), row(ln_mix_b[i]),
                  tm=TOKEN_TILE, to=ATTN_Q_TILE,
                  o_tile=lambda r: r // nq * nq + _paired_tile_slot(r % nq, nq))
    xp, os_hi = ffn_decode(xp, i, slice(hb, db))
    os_ = jnp.concatenate([os_lo, os_hi], axis=0)
    xs = _proj_ln(os_.reshape(db * ds, d), xs, w_attn_out_b, row(ln_mix_g[i]), row(ln_mix_b[i]),
                  tm=db * ds, to=db * ds)
    xs = _ffn_sample(xs, w_ffn_in_b, w_ffn_out_b, i, row(ln_ffn_g[i]), row(ln_ffn_b[i]))

    return (xp.reshape(b, s, d), xs.reshape(db, ds, d), conv_p,
            k_p, v_p.reshape(b, s, N_HEADS, V_DIM),
            conv_s,
            k_s.reshape(db, ds, 2 * N_HEADS, HEAD_DIM), v_s.reshape(db, ds, N_HEADS, V_DIM))
```

```python
import functools
import math
from typing import NamedTuple

import jax
import jax.numpy as jnp
from jax import lax
from jax.experimental import pallas as pl
from jax.experimental.pallas import tpu as pltpu

N_HEADS = 8
HEAD_DIM = 64
V_DIM = 2 * HEAD_DIM
ROT_DIM = HEAD_DIM // 4
ROPE_THETA = 500000.0
CONV_WIDTH = 3
DEPTH = 2
LN_EPS = 1e-5
SUBLN_EPS = 1e-5
ALPHA = (2 * DEPTH) ** 0.25
SCALE = HEAD_DIM ** -0.5
LOG2_E = math.log2(math.e)
ATTN_LAYER = 1
LAM_INIT = 0.8 - 0.6 * math.exp(-0.3 * ATTN_LAYER)

LANES = 128
SUBLANES = 8
BF16_SUBLANES = 16
MXU_DIM = 256
VMEM_LIMIT_BYTES = 56 * 1024 * 1024

TOKEN_TILE = 1024
SUB_TILE = 256
ATTN_Q_TILE = 512
ATTN_KV_TILE = 512
PAGES_PER_STEP = 8

_NT = (((1,), (1,)), ((), ()))


def _bf16(x):
    return x.astype(jnp.bfloat16)


def _layernorm(y, g, b):
    mu = jnp.mean(y, axis=-1, keepdims=True)
    yc = y - mu
    var = jnp.mean(yc * yc, axis=-1, keepdims=True)
    return yc * lax.rsqrt(var + LN_EPS) * g + b


def _const_spec(shape):
    nd = len(shape)
    return pl.BlockSpec(shape, lambda *_: (0,) * nd, pipeline_mode=pl.Buffered(1))


def _params(n_axes, flags=None):
    return pltpu.CompilerParams(
        dimension_semantics=("arbitrary",) * n_axes,
        vmem_limit_bytes=VMEM_LIMIT_BYTES,
        flags=flags)


def _sub_tiles(rows):
    step = min(rows, SUB_TILE)
    return [slice(r, r + step) for r in range(0, rows, step)]


def _conv_prompt_body(x_ref, w_in_ref, taps_ref, w_out_ref, g_ref, b_ref,
                      o_ref, tail_ref, carry_ref, *, tiles_per_seq):
    i = pl.program_id(0)
    d = x_ref.shape[1]

    @pl.when(i % tiles_per_seq == 0)
    def _():
        carry_ref[...] = jnp.zeros_like(carry_ref)

    x = x_ref[...]
    t = x.shape[0]
    h3 = jnp.dot(_bf16(x), w_in_ref[...], preferred_element_type=jnp.float32)
    gb, gc, h = h3[:, :d], h3[:, d:2 * d], h3[:, 2 * d:]
    u = gc * h
    row = lax.broadcasted_iota(jnp.int32, (t, 1), 0)
    c6 = carry_ref[SUBLANES - 2:SUBLANES - 1, :]
    c7 = carry_ref[SUBLANES - 1:SUBLANES, :]
    u1 = jnp.where(row == 0, c7, pltpu.roll(u, 1, 0))
    u2 = jnp.where(row == 0, c6, jnp.where(row == 1, c7, pltpu.roll(u, 2, 0)))
    taps = taps_ref[...]
    conv = taps[0:1, :] * u2 + taps[1:2, :] * u1 + taps[2:3, :] * u
    y = jnp.dot(_bf16(gb * conv), w_out_ref[...], preferred_element_type=jnp.float32)
    o_ref[...] = _layernorm(ALPHA * x + y, g_ref[...], b_ref[...])

    carry_ref[...] = u[t - SUBLANES:, :]

    @pl.when(i % tiles_per_seq == tiles_per_seq - 1)
    def _():
        tail_ref[...] = u[t - SUBLANES:, :]


def _conv_prompt(x, w_in, taps, w_out, g, b, seq_len):
    n, d = x.shape
    tm = TOKEN_TILE
    tiles_per_seq = seq_len // tm
    return pl.pallas_call(
        functools.partial(_conv_prompt_body, tiles_per_seq=tiles_per_seq),
        grid=(n // tm,),
        in_specs=[
            pl.BlockSpec((tm, d), lambda i: (i, 0)),
            _const_spec(w_in.shape), _const_spec(taps.shape), _const_spec(w_out.shape),
            _const_spec(g.shape), _const_spec(b.shape),
        ],
        out_specs=[
            pl.BlockSpec((tm, d), lambda i: (i, 0)),
            pl.BlockSpec((SUBLANES, d), lambda i: (i // tiles_per_seq, 0)),
        ],
        out_shape=[
            jax.ShapeDtypeStruct((n, d), jnp.float32),
            jax.ShapeDtypeStruct((n // seq_len * SUBLANES, d), jnp.float32),
        ],
        scratch_shapes=[pltpu.VMEM((SUBLANES, d), jnp.float32)],
        compiler_params=_params(1),
        name="conv_prompt",
    )(x, w_in, taps, w_out, g, b)


def _conv_sample_body(x_ref, st_ref, w_in_ref, taps_ref, w_out_ref, g_ref, b_ref,
                      o_ref, u_ref):
    x = x_ref[...]
    t, d = x.shape
    h3 = jnp.dot(_bf16(x), w_in_ref[...], preferred_element_type=jnp.float32)
    gb, gc, h = h3[:, :d], h3[:, d:2 * d], h3[:, 2 * d:]
    u = gc * h
    st = st_ref[...]
    pos = lax.broadcasted_iota(jnp.int32, (t, 1), 0) % SUBLANES
    u1 = jnp.where(pos == 0, pltpu.roll(st, t - 1, 0), pltpu.roll(u, 1, 0))
    u2 = jnp.where(pos < 2, st, pltpu.roll(u, 2, 0))
    taps = taps_ref[...]
    conv = taps[0:1, :] * u2 + taps[1:2, :] * u1 + taps[2:3, :] * u
    y = jnp.dot(_bf16(gb * conv), w_out_ref[...], preferred_element_type=jnp.float32)
    o_ref[...] = _layernorm(ALPHA * x + y, g_ref[...], b_ref[...])
    u_ref[...] = u


def _conv_sample(x, st, w_in, taps, w_out, g, b):
    n, d = x.shape
    return pl.pallas_call(
        _conv_sample_body,
        grid=(1,),
        in_specs=[_const_spec(a.shape) for a in (x, st, w_in, taps, w_out, g, b)],
        out_specs=[_const_spec((n, d)), _const_spec((n, d))],
        out_shape=[jax.ShapeDtypeStruct((n, d), jnp.float32)] * 2,
        compiler_params=_params(1),
        name="conv_sample",
    )(x, st, w_in, taps, w_out, g, b)


def _ffn_split(f):
    assert f % MXU_DIM == 0
    cut = (f // MXU_DIM + 1) // 2 * MXU_DIM
    return (0, cut), (cut, f)


def _ffn_part(x, w_in_ref, w_out_ref, lo, hi):
    f = w_out_ref.shape[1]
    xb = _bf16(x)
    gate = jnp.dot(xb, w_in_ref[0, :, lo:hi], preferred_element_type=jnp.float32)
    up = jnp.dot(xb, w_in_ref[0, :, f + lo:f + hi], preferred_element_type=jnp.float32)
    a = gate * jax.nn.sigmoid(gate) * up
    return jnp.dot(_bf16(a), w_out_ref[0, lo:hi, :], preferred_element_type=jnp.float32)


def _ffn_sample_body(x_ref, w_in_ref, w_out_ref, g_ref, b_ref, o_ref):
    x = x_ref[...]
    y = _ffn_part(x, w_in_ref, w_out_ref, 0, w_out_ref.shape[1])
    o_ref[...] = _layernorm(ALPHA * x + y, g_ref[...], b_ref[...])


def _layer_spec(shape, layer):
    nd = len(shape)
    return pl.BlockSpec((1,) + shape[1:], lambda *_: (layer,) + (0,) * (nd - 1),
                        pipeline_mode=pl.Buffered(1))


def _ffn_sample(x, w_in, w_out, layer, g, b):
    n, d = x.shape
    return pl.pallas_call(
        _ffn_sample_body,
        grid=(1,),
        in_specs=[_const_spec(x.shape), _layer_spec(w_in.shape, layer),
                  _layer_spec(w_out.shape, layer), _const_spec(g.shape), _const_spec(b.shape)],
        out_specs=_const_spec((n, d)),
        out_shape=jax.ShapeDtypeStruct((n, d), jnp.float32),
        compiler_params=_params(1),
        name="ffn_sample",
    )(x, w_in, w_out, g, b)


def _rope_tables(pos):
    half = ROT_DIM // 2
    inv = jnp.power(ROPE_THETA, -jnp.arange(0, ROT_DIM, 2, dtype=jnp.float32) / ROT_DIM)
    ang = pos.astype(jnp.float32)[:, None] * inv[None, :]
    cos, sin = jnp.cos(ang), jnp.sin(ang)
    dd = jnp.arange(LANES) % HEAD_DIM
    cos_l = jnp.take(cos, dd % half, axis=1)
    sin_l = jnp.take(sin, dd % half, axis=1)
    c = jnp.where(dd[None, :] < ROT_DIM, cos_l, 1.0)
    s_up = jnp.where(dd[None, :] < half, -sin_l, 0.0)
    s_dn = jnp.where((dd[None, :] >= half) & (dd[None, :] < ROT_DIM), sin_l, 0.0)
    return c, s_up, s_dn


def _rope(x, c, s_up, s_dn):
    half = ROT_DIM // 2
    outs = []
    for g in range(x.shape[1] // LANES):
        xg = x[:, g * LANES:(g + 1) * LANES]
        x_up = pltpu.roll(xg, LANES - half, 1)
        x_dn = pltpu.roll(xg, half, 1)
        outs.append(xg * c + x_up * s_up + x_dn * s_dn)
    return jnp.concatenate(outs, axis=1)


def _qkv_rows(x_ref, w_ref, c_ref, su_ref, sd_ref, rows=slice(None)):
    x = x_ref[rows, :]
    d = x.shape[1]
    h3 = jnp.dot(_bf16(x), w_ref[...], preferred_element_type=jnp.float32)
    c, su, sd = c_ref[rows, :], su_ref[rows, :], sd_ref[rows, :]
    q = _rope(h3[:, :d], c, su, sd)
    k = _rope(h3[:, d:2 * d], c, su, sd)
    return q, k, h3[:, 2 * d:]


def _qkv_sample_body(x_ref, w_ref, c_ref, su_ref, sd_ref, q_ref, k_ref, v_ref):
    q, k, v = _qkv_rows(x_ref, w_ref, c_ref, su_ref, sd_ref)
    q_ref[...] = q * SCALE
    k_ref[...] = k
    v_ref[...] = v


def _qkv_sample(x, w, tables):
    n, d = x.shape
    specs = [_const_spec(a.shape) for a in (x, w) + tuple(tables)]
    return pl.pallas_call(
        _qkv_sample_body,
        grid=(1,),
        in_specs=specs,
        out_specs=[_const_spec((n, d))] * 3,
        out_shape=[jax.ShapeDtypeStruct((n, d), jnp.float32)] * 3,
        compiler_params=_params(1),
        name="qkv_sample",
    )(x, w, *tables)


def _qkv_prompt_body(x_ref, w_ref, c_ref, su_ref, sd_ref,
                     qt_ref, kb_ref, kt_ref, vt_ref, v_ref):
    ta = qt_ref.shape[2]
    for c in range(qt_ref.shape[0]):
        for sub in _sub_tiles(ta):
            rows = slice(c * ta + sub.start, c * ta + sub.stop)
            q, k, v = _qkv_rows(x_ref, w_ref, c_ref, su_ref, sd_ref, rows)
            qt_ref[c, :, sub] = _bf16((q * (SCALE * LOG2_E)).T)
            kb_ref[rows, :] = _bf16(k)
            kt_ref[0, :, rows] = k.T
            vt_ref[c, :, sub] = _bf16(v.T)
            v_ref[rows, :] = v


def _qkv_prompt(x, w, tables, seq_len):
    n, d = x.shape
    tm = TOKEN_TILE
    ta = ATTN_Q_TILE
    assert ATTN_Q_TILE == ATTN_KV_TILE and tm % ta == 0
    tps = seq_len // tm
    tspec = pl.BlockSpec((tm, LANES), lambda i: (i % tps, 0))
    row = pl.BlockSpec((tm, d), lambda i: (i, 0))
    blk = pl.BlockSpec((tm // ta, d, ta), lambda i: (i, 0, 0))
    return pl.pallas_call(
        _qkv_prompt_body,
        grid=(n // tm,),
        in_specs=[row, _const_spec(w.shape), tspec, tspec, tspec],
        out_specs=[blk, row, pl.BlockSpec((1, d, tm), lambda i: (i // tps, 0, i % tps)), blk, row],
        out_shape=[
            jax.ShapeDtypeStruct((n // ta, d, ta), jnp.bfloat16),
            jax.ShapeDtypeStruct((n, d), jnp.bfloat16),
            jax.ShapeDtypeStruct((n // seq_len, d, seq_len), jnp.float32),
            jax.ShapeDtypeStruct((n // ta, d, ta), jnp.bfloat16),
            jax.ShapeDtypeStruct((n, d), jnp.float32),
        ],
        compiler_params=_params(1),
        name="qkv_prompt",
    )(x, w, *tables)


def _lambda(lamv_ref):
    lv = lamv_ref[...]
    d1 = jnp.sum(lv[0:1, :] * lv[1:2, :], axis=-1, keepdims=True)
    d2 = jnp.sum(lv[2:3, :] * lv[3:4, :], axis=-1, keepdims=True)
    return jnp.exp(d1) - jnp.exp(d2) + LAM_INIT


def _subln(o, g):
    o = o * lax.rsqrt(jnp.mean(o * o, axis=-1, keepdims=True) + SUBLN_EPS)
    return o * g * (1.0 - LAM_INIT)


def _prompt_attn_body(qlo_ref, qhi_ref, k_ref, vt_ref, lamv_ref, gcol_ref, o_ref,
                      q_ref, s0_ref, s1_ref, bmax_ref, m_ref, acc_ref, *, n_tiles):
    tq, tk = ATTN_Q_TILE, ATTN_KV_TILE
    t_lo = pl.program_id(2)
    t_hi = n_tiles - 1 - t_lo
    neg = jnp.finfo(jnp.float32).min
    feat = lax.broadcasted_iota(jnp.int32, (2 * HEAD_DIM, 1), 0)
    for tile, ref in enumerate((qlo_ref, qhi_ref)):
        q_ref[tile, 0] = jnp.where(feat < HEAD_DIM, ref[0], 0)
        q_ref[tile, 1] = jnp.where(feat >= HEAD_DIM, ref[0], 0)
    ones = jnp.ones((acc_ref.shape[2] - V_DIM, tk), jnp.bfloat16)
    acc_ref[...] = jnp.zeros_like(acc_ref)
    m_ref[...] = jnp.full_like(m_ref, neg)
    s_refs = (s0_ref, s1_ref)

    def block(u):
        if u == 0:
            return 0, t_lo, True
        if u == 1:
            return 1, t_hi, True
        n = u - 2
        return jnp.where(n < t_lo, 0, 1), jnp.where(n < t_lo, n, n - t_lo), False

    def scores(u):
        tile, kv, masked = block(u)
        kblk = k_ref[0, pl.ds(pl.multiple_of(kv * tk, tk), tk), :]
        for c in range(2):
            st = jnp.dot(kblk, q_ref[tile, c], preferred_element_type=jnp.float32)
            if masked:
                key = lax.broadcasted_iota(jnp.int32, (tk, tq), 0)
                qry = lax.broadcasted_iota(jnp.int32, (tk, tq), 1)
                st = jnp.where(key <= qry, st, neg)
            s_refs[u % 2][c] = st
            bmax_ref[u % 2, c] = jnp.max(st, axis=0, keepdims=True)

    def accumulate(u):
        tile, kv, _ = block(u)
        s_ref = s_refs[u % 2]
        vext = jnp.concatenate([vt_ref[0, kv], ones], axis=0)
        for c in range(2):
            m_old = m_ref[tile, c]
            m_new = jnp.maximum(m_old, bmax_ref[u % 2, c])
            alpha = jnp.exp2(m_old - m_new)
            m_ref[tile, c] = m_new
            p = _bf16(jnp.exp2(s_ref[c] - m_new))
            acc_ref[tile, c] = alpha * acc_ref[tile, c] + jnp.dot(
                vext, p, preferred_element_type=jnp.float32)

    n_blocks = n_tiles + 1
    scores(0)
    for u in range(n_blocks):
        if u + 1 < n_blocks:
            scores(u + 1)
        accumulate(u)

    lam = _lambda(lamv_ref)
    for tile in range(2):
        o = (acc_ref[tile, 0, :V_DIM] / acc_ref[tile, 0, V_DIM:V_DIM + 1]
             - lam * (acc_ref[tile, 1, :V_DIM] / acc_ref[tile, 1, V_DIM:V_DIM + 1]))
        o = o * lax.rsqrt(jnp.mean(o * o, axis=0, keepdims=True) + SUBLN_EPS)
        o = o * gcol_ref[...] * (1.0 - LAM_INIT)
        o_ref[0, tile, 0] = o.T.astype(o_ref.dtype)


def _prompt_attn(qt, k, vt, lamv, gcol):
    b, s, d = k.shape
    tq, tk = ATTN_Q_TILE, ATTN_KV_TILE
    nq = s // tq
    assert tq == tk and s % tq == 0 and nq % 2 == 0
    return pl.pallas_call(
        functools.partial(_prompt_attn_body, n_tiles=nq),
        grid=(b, N_HEADS, nq // 2),
        in_specs=[
            pl.BlockSpec((1, V_DIM, tq), lambda bi, j, t: (bi * nq + t, j, 0)),
            pl.BlockSpec((1, V_DIM, tq), lambda bi, j, t: (bi * nq + nq - 1 - t, j, 0)),
            pl.BlockSpec((1, s, V_DIM), lambda bi, j, t: (bi, 0, j)),
            pl.BlockSpec((1, s // tk, V_DIM, tk), lambda bi, j, t: (bi, 0, j, 0)),
            _const_spec(lamv.shape), _const_spec(gcol.shape),
        ],
        out_specs=pl.BlockSpec((1, 2, 1, tq, V_DIM), lambda bi, j, t: (bi, 0, t, 0, j)),
        out_shape=jax.ShapeDtypeStruct((b, 2, nq // 2, tq, d), jnp.bfloat16),
        scratch_shapes=[
            pltpu.VMEM((2, 2, V_DIM, tq), jnp.bfloat16),
            pltpu.VMEM((2, tk, tq), jnp.float32),
            pltpu.VMEM((2, tk, tq), jnp.float32),
            pltpu.VMEM((2, 2, 1, tq), jnp.float32),
            pltpu.VMEM((2, 2, 1, tq), jnp.float32),
            pltpu.VMEM((2, 2, V_DIM + BF16_SUBLANES, tq), jnp.float32),
        ],
        compiler_params=_params(3),
        name="prompt_attn",
    )(qt, qt, k, vt, lamv, gcol)


def _paired_tile_slot(t, nq):
    return jnp.where(t < nq // 2, t, nq // 2 + nq - 1 - t)


class _DecodeRefs(NamedTuple):
    q: object
    kn: object
    vn: object
    lamv: object
    g: object
    k_pages: tuple
    v_pages: tuple
    out: object
    m: object
    l: object
    acc: object
    s: object
    alpha: object


def _decode_math(r):
    ds = r.q.shape[2]
    q3 = _bf16(r.q[0])

    def pair_rows(s):
        return s.reshape(N_HEADS, 2 * ds, s.shape[2])

    def scores(k3):
        return pair_rows(jnp.einsum("hqd,htd->hqt", q3, k3, preferred_element_type=jnp.float32))

    def scores_t(kt3):
        return pair_rows(jnp.einsum("hqd,hdt->hqt", q3, kt3, preferred_element_type=jnp.float32))

    def weighted(p, v3):
        return jnp.einsum("jrt,jte->jre", _bf16(p), v3, preferred_element_type=jnp.float32)

    return ds, scores, scores_t, weighted


def _decode_init(r):
    ds, scores, _, weighted = _decode_math(r)
    page = r.v_pages[0].shape[1] // N_HEADS
    kn, vn = r.kn[0], r.vn[0]
    kn3 = _bf16(jnp.concatenate(
        [kn, jnp.zeros((kn.shape[0], page - ds, kn.shape[2]), jnp.float32)], axis=1))
    vn3 = _bf16(jnp.concatenate(
        [vn, jnp.zeros((vn.shape[0], page - ds, vn.shape[2]), jnp.float32)], axis=1))
    s = scores(kn3)
    qi = lax.broadcasted_iota(jnp.int32, s.shape, 1) % ds
    tt = lax.broadcasted_iota(jnp.int32, s.shape, 2)
    s = jnp.where(tt <= qi, s, jnp.finfo(jnp.float32).min)
    m = jnp.max(s, axis=-1, keepdims=True)
    p = jnp.exp(s - m)
    r.m[...] = m
    r.l[...] = jnp.sum(p, axis=-1, keepdims=True)
    r.acc[...] = weighted(p, vn3)


def _decode_scores(r):
    _, _, scores_t, _ = _decode_math(r)
    r.s[...] = scores_t(jnp.concatenate([_bf16(ref[0]) for ref in r.k_pages], axis=2))


def _decode_softmax(r):
    s = r.s[...]
    m_old = r.m[...]
    m_new = jnp.maximum(m_old, jnp.max(s, axis=-1, keepdims=True))
    alpha = jnp.exp(m_old - m_new)
    p = jnp.exp(s - m_new)
    r.l[...] = alpha * r.l[...] + jnp.sum(p, axis=-1, keepdims=True)
    r.m[...] = m_new
    r.alpha[...] = alpha
    r.s[...] = p


def _decode_accumulate(r):
    _, _, _, weighted = _decode_math(r)
    page = r.v_pages[0].shape[1] // N_HEADS
    v3 = jnp.stack([
        jnp.concatenate([_bf16(ref[0, pl.ds(j, page, stride=N_HEADS), :]) for ref in r.v_pages],
                        axis=0)
        for j in range(N_HEADS)])
    r.acc[...] = r.alpha[...] * r.acc[...] + weighted(r.s[...], v3)


def _decode_finish(r):
    ds = r.q.shape[2]
    o = r.acc[...] / r.l[...]
    o = o[:, :ds, :] - _lambda(r.lamv) * o[:, ds:, :]
    o = _subln(o, r.g[...])
    for j in range(N_HEADS):
        r.out[0, :, j * V_DIM:(j + 1) * V_DIM] = o[j]


def _ffn_decode_body(pt_ref, x_ref, w_in_ref, w_out_ref, g_ref, b_ref,
                     q_ref, kn_ref, vn_ref, lamv_ref, gsub_ref, *refs, chunks_per_seq):
    pps = PAGES_PER_STEP
    o_ref, od_ref, y_ref = refs[2 * pps:2 * pps + 3]
    dec = _DecodeRefs(q_ref, kn_ref, vn_ref, lamv_ref, gsub_ref,
                      refs[:pps], refs[pps:2 * pps], od_ref, *refs[2 * pps + 3:])
    s = pl.program_id(0)
    half = s % 2
    chunk = s % chunks_per_seq

    @pl.when(chunk == 0)
    def _():
        _decode_init(dec)

    def step(h_static):
        x = x_ref[...]
        _decode_scores(dec)
        y = _ffn_part(x, w_in_ref, w_out_ref, *_ffn_split(w_out_ref.shape[1])[h_static])
        _decode_softmax(dec)
        _decode_accumulate(dec)
        if h_static == 0:
            y_ref[...] = y
        else:
            o_ref[...] = _layernorm(ALPHA * x + (y_ref[...] + y), g_ref[...], b_ref[...])

    for h_static in range(2):
        pl.when(half == h_static)(functools.partial(step, h_static))

    @pl.when(chunk == chunks_per_seq - 1)
    def _():
        _decode_finish(dec)


def _ffn_decode(x, w_in, w_out, layer, g, b,
                q3, k_new3, v_new3, cache_kt, cache_v, page_table, lamv, gsub):
    n, d = x.shape
    sb, nh2, ds, hd = q3.shape
    n_pages = page_table.shape[1]
    page = cache_v.shape[1] // N_HEADS
    pps = PAGES_PER_STEP
    chunks = n_pages // pps
    rows = SUB_TILE
    n_steps = 2 * (n // rows)
    assert n_pages % pps == 0 and ds == SUBLANES and n_steps == sb * chunks

    def page_spec(shape, pg):
        return pl.BlockSpec(
            (1,) + shape[1:],
            lambda s, pt: (pt[s // chunks * n_pages + s % chunks * pps + pg],)
            + (0,) * (len(shape) - 1))

    def seq_spec(shape):
        return pl.BlockSpec((1,) + shape[1:],
                            lambda s, pt: (s // chunks,) + (0,) * (len(shape) - 1))

    row_spec = pl.BlockSpec((rows, d), lambda s, pt: (s // 2, 0))
    grid_spec = pltpu.PrefetchScalarGridSpec(
        num_scalar_prefetch=1,
        grid=(n_steps,),
        in_specs=[row_spec, _layer_spec(w_in.shape, layer), _layer_spec(w_out.shape, layer),
                  _const_spec(g.shape), _const_spec(b.shape),
                  seq_spec(q3.shape), seq_spec(k_new3.shape), seq_spec(v_new3.shape),
                  _const_spec(lamv.shape), _const_spec(gsub.shape)]
                 + [page_spec(cache_kt.shape, pg) for pg in range(pps)]
                 + [page_spec(cache_v.shape, pg) for pg in range(pps)],
        out_specs=[row_spec, seq_spec((sb, ds, d))],
        scratch_shapes=[
            pltpu.VMEM((rows, d), jnp.float32),
            pltpu.VMEM((N_HEADS, 2 * ds, 1), jnp.float32),
            pltpu.VMEM((N_HEADS, 2 * ds, 1), jnp.float32),
            pltpu.VMEM((N_HEADS, 2 * ds, V_DIM), jnp.float32),
            pltpu.VMEM((N_HEADS, 2 * ds, pps * page), jnp.float32),
            pltpu.VMEM((N_HEADS, 2 * ds, 1), jnp.float32),
        ],
    )
    return pl.pallas_call(
        functools.partial(_ffn_decode_body, chunks_per_seq=chunks),
        grid_spec=grid_spec,
        out_shape=[jax.ShapeDtypeStruct((n, d), jnp.float32),
                   jax.ShapeDtypeStruct((sb, ds, d), jnp.float32)],
        compiler_params=_params(1),
        name="ffn_decode",
    )(page_table.reshape(-1), x, w_in, w_out, g, b, q3, k_new3, v_new3, lamv, gsub,
      *([cache_kt] * pps), *([cache_v] * pps))


def _proj_ln_body(*refs):
    o_refs, (x_ref, w_ref, g_ref, b_ref, y_ref) = refs[:-5], refs[-5:]
    to = o_refs[0].shape[0]
    for n, o_ref in enumerate(o_refs):
        for sub in _sub_tiles(to):
            rows = slice(n * to + sub.start, n * to + sub.stop)
            y = jnp.dot(_bf16(o_ref[sub, :]), w_ref[...], preferred_element_type=jnp.float32)
            y_ref[rows, :] = _layernorm(ALPHA * x_ref[rows, :] + y, g_ref[...], b_ref[...])


def _proj_ln(o, x, w, g, b, tm, to, o_tile=lambda i: i):
    n, d = x.shape
    assert n % tm == 0 and tm % to == 0
    per = tm // to
    row = pl.BlockSpec((tm, d), lambda i: (i, 0))
    o_specs = [pl.BlockSpec((to, d), lambda i, k=k: (o_tile(i * per + k), 0)) for k in range(per)]
    return pl.pallas_call(
        _proj_ln_body,
        grid=(n // tm,),
        in_specs=o_specs + [row, _const_spec(w.shape), _const_spec(g.shape), _const_spec(b.shape)],
        out_specs=row,
        out_shape=jax.ShapeDtypeStruct((n, d), jnp.float32),
        compiler_params=_params(1),
        name="proj_ln",
    )(*([o] * per), x, w, g, b)


def kernel(x_prompt, x_sample, state_conv, cache_k, cache_v, page_table, w_conv_in, w_conv, w_conv_out, w_qkv, lambda_q1, lambda_k1, lambda_q2, lambda_k2, subln_g, w_attn_out, ln_mix_g, ln_mix_b, w_ffn_in, w_ffn_out, ln_ffn_g, ln_ffn_b):
    b, s, d = x_prompt.shape
    db, ds, _ = x_sample.shape
    n_pool, page = cache_k.shape[:2]
    past_len = page_table.shape[1] * page
    f32 = jnp.float32
    assert ds == SUBLANES and s % TOKEN_TILE == 0

    w_conv_in_b, w_conv_out_b = _bf16(w_conv_in), _bf16(w_conv_out)
    w_qkv_b, w_attn_out_b = _bf16(w_qkv), _bf16(w_attn_out)
    w_ffn_in_b, w_ffn_out_b = _bf16(w_ffn_in), _bf16(w_ffn_out)
    taps = w_conv.astype(f32)
    row = lambda a: a.reshape(1, -1).astype(f32)
    lamv = jnp.stack([lambda_q1, lambda_k1, lambda_q2, lambda_k2]).astype(f32)
    g_sub = row(subln_g)

    xp = x_prompt.reshape(b * s, d)
    xs = x_sample.reshape(db * ds, d)

    i = 0
    xp, tail_p = _conv_prompt(xp, w_conv_in_b, taps, w_conv_out_b,
                              row(ln_mix_g[i]), row(ln_mix_b[i]), s)
    conv_p = tail_p.reshape(b, SUBLANES, d)[:, SUBLANES - (CONV_WIDTH - 1):]
    st = jnp.pad(state_conv, ((0, 0), (0, ds - (CONV_WIDTH - 1)), (0, 0))).reshape(db * ds, d)
    xs, u_s = _conv_sample(xs, st, w_conv_in_b, taps, w_conv_out_b,
                           row(ln_mix_g[i]), row(ln_mix_b[i]))
    conv_s = u_s.reshape(db, ds, d)[:, ds - (CONV_WIDTH - 1):]
    xs = _ffn_sample(xs, w_ffn_in_b, w_ffn_out_b, i, row(ln_ffn_g[i]), row(ln_ffn_b[i]))

    tab_s = _rope_tables(past_len + jnp.arange(db * ds) % ds)
    qs, k_s, v_s = _qkv_sample(xs, w_qkv_b, tab_s)
    heads_first = lambda a, nh: a.reshape(db, ds, nh, d // nh).transpose(0, 2, 1, 3)
    q3, k_new3, v_new3 = (heads_first(qs, 2 * N_HEADS), heads_first(k_s, 2 * N_HEADS),
                          heads_first(v_s, N_HEADS))
    cache_kt = cache_k.transpose(0, 2, 3, 1)
    cache_vr = cache_v.reshape(n_pool, page * N_HEADS, V_DIM)
    hb = db // 2

    def ffn_decode(x, layer, seqs):
        return _ffn_decode(x, w_ffn_in_b, w_ffn_out_b, layer,
                           row(ln_ffn_g[layer]), row(ln_ffn_b[layer]),
                           q3[seqs], k_new3[seqs], v_new3[seqs], cache_kt, cache_vr,
                           page_table[seqs], lamv, g_sub)

    xp, os_lo = ffn_decode(xp, i, slice(0, hb))

    i = 1
    tab_p = _rope_tables(jnp.arange(s))
    qt_p, kb_p, kt_p, vt_p, v_p = _qkv_prompt(xp, w_qkv_b, tab_p, s)
    k_p = kt_p.reshape(b, 2 * N_HEADS, HEAD_DIM, s).transpose(0, 3, 1, 2)

    op = _prompt_attn(qt_p, kb_p.reshape(b, s, d),
                      vt_p.reshape(b, s // ATTN_KV_TILE, d, ATTN_KV_TILE),
                      lamv, g_sub.reshape(-1, 1))
    nq = s // ATTN_Q_TILE
    xp = _proj_ln(op.reshape(b * s, d), xp, w_attn_out_b, row(ln_mix_g[i]), row(ln_mix_b[i]),
                  tm=TOKEN_TILE, to=ATTN_Q_TILE,
                  o_tile=lambda r: r // nq * nq + _paired_tile_slot(r % nq, nq))
    xp, os_hi = ffn_decode(xp, i, slice(hb, db))
    os_ = jnp.concatenate([os_lo, os_hi], axis=0)
    xs = _proj_ln(os_.reshape(db * ds, d), xs, w_attn_out_b, row(ln_mix_g[i]), row(ln_mix_b[i]),
                  tm=db * ds, to=db * ds)
    xs = _ffn_sample(xs, w_ffn_in_b, w_ffn_out_b, i, row(ln_ffn_g[i]), row(ln_ffn_b[i]))

    return (xp.reshape(b, s, d), xs.reshape(db, ds, d), conv_p,
            k_p, v_p.reshape(b, s, N_HEADS, V_DIM),
            conv_s,
            k_s.reshape(db, ds, 2 * N_HEADS, HEAD_DIM), v_s.reshape(db, ds, N_HEADS, V_DIM))
```

```python
import functools
import math
from typing import NamedTuple

import jax
import jax.numpy as jnp
from jax import lax
from jax.experimental import pallas as pl
from jax.experimental.pallas import tpu as pltpu

N_HEADS = 8
HEAD_DIM = 64
V_DIM = 2 * HEAD_DIM
ROT_DIM = HEAD_DIM // 4
ROPE_THETA = 500000.0
CONV_WIDTH = 3
DEPTH = 2
LN_EPS = 1e-5
SUBLN_EPS = 1e-5
ALPHA = (2 * DEPTH) ** 0.25
SCALE = HEAD_DIM ** -0.5
LOG2_E = math.log2(math.e)
ATTN_LAYER = 1
LAM_INIT = 0.8 - 0.6 * math.exp(-0.3 * ATTN_LAYER)

LANES = 128
SUBLANES = 8
BF16_SUBLANES = 16
MXU_DIM = 256
VMEM_LIMIT_BYTES = 56 * 1024 * 1024

TOKEN_TILE = 1024
SUB_TILE = 256
ATTN_Q_TILE = 512
ATTN_KV_TILE = 512
PAGES_PER_STEP = 8

_NT = (((1,), (1,)), ((), ()))


def _bf16(x):
    return x.astype(jnp.bfloat16)


def _layernorm(y, g, b):
    mu = jnp.mean(y, axis=-1, keepdims=True)
    yc = y - mu
    var = jnp.mean(yc * yc, axis=-1, keepdims=True)
    return yc * lax.rsqrt(var + LN_EPS) * g + b


def _const_spec(shape):
    nd = len(shape)
    return pl.BlockSpec(shape, lambda *_: (0,) * nd, pipeline_mode=pl.Buffered(1))


def _params(n_axes, flags=None):
    return pltpu.CompilerParams(
        dimension_semantics=("arbitrary",) * n_axes,
        vmem_limit_bytes=VMEM_LIMIT_BYTES,
        flags=flags)


def _sub_tiles(rows):
    step = min(rows, SUB_TILE)
    return [slice(r, r + step) for r in range(0, rows, step)]


def _conv_prompt_body(x_ref, w_in_ref, taps_ref, w_out_ref, g_ref, b_ref,
                      o_ref, tail_ref, carry_ref, *, tiles_per_seq):
    i = pl.program_id(0)
    d = x_ref.shape[1]

    @pl.when(i % tiles_per_seq == 0)
    def _():
        carry_ref[...] = jnp.zeros_like(carry_ref)

    x = x_ref[...]
    t = x.shape[0]
    h3 = jnp.dot(_bf16(x), w_in_ref[...], preferred_element_type=jnp.float32)
    gb, gc, h = h3[:, :d], h3[:, d:2 * d], h3[:, 2 * d:]
    u = gc * h
    row = lax.broadcasted_iota(jnp.int32, (t, 1), 0)
    c6 = carry_ref[SUBLANES - 2:SUBLANES - 1, :]
    c7 = carry_ref[SUBLANES - 1:SUBLANES, :]
    u1 = jnp.where(row == 0, c7, pltpu.roll(u, 1, 0))
    u2 = jnp.where(row == 0, c6, jnp.where(row == 1, c7, pltpu.roll(u, 2, 0)))
    taps = taps_ref[...]
    conv = taps[0:1, :] * u2 + taps[1:2, :] * u1 + taps[2:3, :] * u
    y = jnp.dot(_bf16(gb * conv), w_out_ref[...], preferred_element_type=jnp.float32)
    o_ref[...] = _layernorm(ALPHA * x + y, g_ref[...], b_ref[...])

    carry_ref[...] = u[t - SUBLANES:, :]

    @pl.when(i % tiles_per_seq == tiles_per_seq - 1)
    def _():
        tail_ref[...] = u[t - SUBLANES:, :]


def _conv_prompt(x, w_in, taps, w_out, g, b, seq_len):
    n, d = x.shape
    tm = TOKEN_TILE
    tiles_per_seq = seq_len // tm
    return pl.pallas_call(
        functools.partial(_conv_prompt_body, tiles_per_seq=tiles_per_seq),
        grid=(n // tm,),
        in_specs=[
            pl.BlockSpec((tm, d), lambda i: (i, 0)),
            _const_spec(w_in.shape), _const_spec(taps.shape), _const_spec(w_out.shape),
            _const_spec(g.shape), _const_spec(b.shape),
        ],
        out_specs=[
            pl.BlockSpec((tm, d), lambda i: (i, 0)),
            pl.BlockSpec((SUBLANES, d), lambda i: (i // tiles_per_seq, 0)),
        ],
        out_shape=[
            jax.ShapeDtypeStruct((n, d), jnp.float32),
            jax.ShapeDtypeStruct((n // seq_len * SUBLANES, d), jnp.float32),
        ],
        scratch_shapes=[pltpu.VMEM((SUBLANES, d), jnp.float32)],
        compiler_params=_params(1),
        name="conv_prompt",
    )(x, w_in, taps, w_out, g, b)


def _conv_sample_body(x_ref, st_ref, w_in_ref, taps_ref, w_out_ref, g_ref, b_ref,
                      o_ref, u_ref):
    x = x_ref[...]
    t, d = x.shape
    h3 = jnp.dot(_bf16(x), w_in_ref[...], preferred_element_type=jnp.float32)
    gb, gc, h = h3[:, :d], h3[:, d:2 * d], h3[:, 2 * d:]
    u = gc * h
    st = st_ref[...]
    pos = lax.broadcasted_iota(jnp.int32, (t, 1), 0) % SUBLANES
    u1 = jnp.where(pos == 0, pltpu.roll(st, t - 1, 0), pltpu.roll(u, 1, 0))
    u2 = jnp.where(pos < 2, st, pltpu.roll(u, 2, 0))
    taps = taps_ref[...]
    conv = taps[0:1, :] * u2 + taps[1:2, :] * u1 + taps[2:3, :] * u
    y = jnp.dot(_bf16(gb * conv), w_out_ref[...], preferred_element_type=jnp.float32)
    o_ref[...] = _layernorm(ALPHA * x + y, g_ref[...], b_ref[...])
    u_ref[...] = u


def _conv_sample(x, st, w_in, taps, w_out, g, b):
    n, d = x.shape
    return pl.pallas_call(
        _conv_sample_body,
        grid=(1,),
        in_specs=[_const_spec(a.shape) for a in (x, st, w_in, taps, w_out, g, b)],
        out_specs=[_const_spec((n, d)), _const_spec((n, d))],
        out_shape=[jax.ShapeDtypeStruct((n, d), jnp.float32)] * 2,
        compiler_params=_params(1),
        name="conv_sample",
    )(x, st, w_in, taps, w_out, g, b)


def _ffn_split(f):
    assert f % MXU_DIM == 0
    cut = (f // MXU_DIM + 1) // 2 * MXU_DIM
    return (0, cut), (cut, f)


def _ffn_part(x, w_in_ref, w_out_ref, lo, hi):
    f = w_out_ref.shape[1]
    xb = _bf16(x)
    gate = jnp.dot(xb, w_in_ref[0, :, lo:hi], preferred_element_type=jnp.float32)
    up = jnp.dot(xb, w_in_ref[0, :, f + lo:f + hi], preferred_element_type=jnp.float32)
    a = gate * jax.nn.sigmoid(gate) * up
    return jnp.dot(_bf16(a), w_out_ref[0, lo:hi, :], preferred_element_type=jnp.float32)


def _ffn_sample_body(x_ref, w_in_ref, w_out_ref, g_ref, b_ref, o_ref):
    x = x_ref[...]
    y = _ffn_part(x, w_in_ref, w_out_ref, 0, w_out_ref.shape[1])
    o_ref[...] = _layernorm(ALPHA * x + y, g_ref[...], b_ref[...])


def _layer_spec(shape, layer):
    nd = len(shape)
    return pl.BlockSpec((1,) + shape[1:], lambda *_: (layer,) + (0,) * (nd - 1),
                        pipeline_mode=pl.Buffered(1))


def _ffn_sample(x, w_in, w_out, layer, g, b):
    n, d = x.shape
    return pl.pallas_call(
        _ffn_sample_body,
        grid=(1,),
        in_specs=[_const_spec(x.shape), _layer_spec(w_in.shape, layer),
                  _layer_spec(w_out.shape, layer), _const_spec(g.shape), _const_spec(b.shape)],
        out_specs=_const_spec((n, d)),
        out_shape=jax.ShapeDtypeStruct((n, d), jnp.float32),
        compiler_params=_params(1),
        name="ffn_sample",
    )(x, w_in, w_out, g, b)


def _rope_tables(pos):
    half = ROT_DIM // 2
    inv = jnp.power(ROPE_THETA, -jnp.arange(0, ROT_DIM, 2, dtype=jnp.float32) / ROT_DIM)
    ang = pos.astype(jnp.float32)[:, None] * inv[None, :]
    cos, sin = jnp.cos(ang), jnp.sin(ang)
    dd = jnp.arange(LANES) % HEAD_DIM
    cos_l = jnp.take(cos, dd % half, axis=1)
    sin_l = jnp.take(sin, dd % half, axis=1)
    c = jnp.where(dd[None, :] < ROT_DIM, cos_l, 1.0)
    s_up = jnp.where(dd[None, :] < half, -sin_l, 0.0)
    s_dn = jnp.where((dd[None, :] >= half) & (dd[None, :] < ROT_DIM), sin_l, 0.0)
    return c, s_up, s_dn


def _rope(x, c, s_up, s_dn):
    half = ROT_DIM // 2
    outs = []
    for g in range(x.shape[1] // LANES):
        xg = x[:, g * LANES:(g + 1) * LANES]
        x_up = pltpu.roll(xg, LANES - half, 1)
        x_dn = pltpu.roll(xg, half, 1)
        outs.append(xg * c + x_up * s_up + x_dn * s_dn)
    return jnp.concatenate(outs, axis=1)


def _qkv_rows(x_ref, w_ref, c_ref, su_ref, sd_ref, rows=slice(None)):
    x = x_ref[rows, :]
    d = x.shape[1]
    h3 = jnp.dot(_bf16(x), w_ref[...], preferred_element_type=jnp.float32)
    c, su, sd = c_ref[rows, :], su_ref[rows, :], sd_ref[rows, :]
    q = _rope(h3[:, :d], c, su, sd)
    k = _rope(h3[:, d:2 * d], c, su, sd)
    return q, k, h3[:, 2 * d:]


def _qkv_sample_body(x_ref, w_ref, c_ref, su_ref, sd_ref, q_ref, k_ref, v_ref):
    q, k, v = _qkv_rows(x_ref, w_ref, c_ref, su_ref, sd_ref)
    q_ref[...] = q * SCALE
    k_ref[...] = k
    v_ref[...] = v


def _qkv_sample(x, w, tables):
    n, d = x.shape
    specs = [_const_spec(a.shape) for a in (x, w) + tuple(tables)]
    return pl.pallas_call(
        _qkv_sample_body,
        grid=(1,),
        in_specs=specs,
        out_specs=[_const_spec((n, d))] * 3,
        out_shape=[jax.ShapeDtypeStruct((n, d), jnp.float32)] * 3,
        compiler_params=_params(1),
        name="qkv_sample",
    )(x, w, *tables)


def _qkv_prompt_body(x_ref, w_ref, c_ref, su_ref, sd_ref,
                     qt_ref, kb_ref, kt_ref, vt_ref, v_ref):
    ta = qt_ref.shape[2]
    for c in range(qt_ref.shape[0]):
        for sub in _sub_tiles(ta):
            rows = slice(c * ta + sub.start, c * ta + sub.stop)
            q, k, v = _qkv_rows(x_ref, w_ref, c_ref, su_ref, sd_ref, rows)
            qt_ref[c, :, sub] = _bf16((q * (SCALE * LOG2_E)).T)
            kb_ref[rows, :] = _bf16(k)
            kt_ref[0, :, rows] = k.T
            vt_ref[c, :, sub] = _bf16(v.T)
            v_ref[rows, :] = v


def _qkv_prompt(x, w, tables, seq_len):
    n, d = x.shape
    tm = TOKEN_TILE
    ta = ATTN_Q_TILE
    assert ATTN_Q_TILE == ATTN_KV_TILE and tm % ta == 0
    tps = seq_len // tm
    tspec = pl.BlockSpec((tm, LANES), lambda i: (i % tps, 0))
    row = pl.BlockSpec((tm, d), lambda i: (i, 0))
    blk = pl.BlockSpec((tm // ta, d, ta), lambda i: (i, 0, 0))
    return pl.pallas_call(
        _qkv_prompt_body,
        grid=(n // tm,),
        in_specs=[row, _const_spec(w.shape), tspec, tspec, tspec],
        out_specs=[blk, row, pl.BlockSpec((1, d, tm), lambda i: (i // tps, 0, i % tps)), blk, row],
        out_shape=[
            jax.ShapeDtypeStruct((n // ta, d, ta), jnp.bfloat16),
            jax.ShapeDtypeStruct((n, d), jnp.bfloat16),
            jax.ShapeDtypeStruct((n // seq_len, d, seq_len), jnp.float32),
            jax.ShapeDtypeStruct((n // ta, d, ta), jnp.bfloat16),
            jax.ShapeDtypeStruct((n, d), jnp.float32),
        ],
        compiler_params=_params(1),
        name="qkv_prompt",
    )(x, w, *tables)


def _lambda(lamv_ref):
    lv = lamv_ref[...]
    d1 = jnp.sum(lv[0:1, :] * lv[1:2, :], axis=-1, keepdims=True)
    d2 = jnp.sum(lv[2:3, :] * lv[3:4, :], axis=-1, keepdims=True)
    return jnp.exp(d1) - jnp.exp(d2) + LAM_INIT


def _subln(o, g):
    o = o * lax.rsqrt(jnp.mean(o * o, axis=-1, keepdims=True) + SUBLN_EPS)
    return o * g * (1.0 - LAM_INIT)


def _prompt_attn_body(qlo_ref, qhi_ref, k_ref, vt_ref, lamv_ref, gcol_ref, o_ref,
                      q_ref, s0_ref, s1_ref, bmax_ref, m_ref, acc_ref, *, n_tiles):
    tq, tk = ATTN_Q_TILE, ATTN_KV_TILE
    t_lo = pl.program_id(2)
    t_hi = n_tiles - 1 - t_lo
    neg = jnp.finfo(jnp.float32).min
    feat = lax.broadcasted_iota(jnp.int32, (2 * HEAD_DIM, 1), 0)
    for tile, ref in enumerate((qlo_ref, qhi_ref)):
        q_ref[tile, 0] = jnp.where(feat < HEAD_DIM, ref[0], 0)
        q_ref[tile, 1] = jnp.where(feat >= HEAD_DIM, ref[0], 0)
    ones = jnp.ones((acc_ref.shape[2] - V_DIM, tk), jnp.bfloat16)
    acc_ref[...] = jnp.zeros_like(acc_ref)
    m_ref[...] = jnp.full_like(m_ref, neg)
    s_refs = (s0_ref, s1_ref)

    def block(u):
        if u == 0:
            return 0, t_lo, True
        if u == 1:
            return 1, t_hi, True
        n = u - 2
        return jnp.where(n < t_lo, 0, 1), jnp.where(n < t_lo, n, n - t_lo), False

    def scores(u):
        tile, kv, masked = block(u)
        kblk = k_ref[0, pl.ds(pl.multiple_of(kv * tk, tk), tk), :]
        for c in range(2):
            st = jnp.dot(kblk, q_ref[tile, c], preferred_element_type=jnp.float32)
            if masked:
                key = lax.broadcasted_iota(jnp.int32, (tk, tq), 0)
                qry = lax.broadcasted_iota(jnp.int32, (tk, tq), 1)
                st = jnp.where(key <= qry, st, neg)
            s_refs[u % 2][c] = st
            bmax_ref[u % 2, c] = jnp.max(st, axis=0, keepdims=True)

    def accumulate(u):
        tile, kv, _ = block(u)
        s_ref = s_refs[u % 2]
        vext = jnp.concatenate([vt_ref[0, kv], ones], axis=0)
        for c in range(2):
            m_old = m_ref[tile, c]
            m_new = jnp.maximum(m_old, bmax_ref[u % 2, c])
            alpha = jnp.exp2(m_old - m_new)
            m_ref[tile, c] = m_new
            p = _bf16(jnp.exp2(s_ref[c] - m_new))
            acc_ref[tile, c] = alpha * acc_ref[tile, c] + jnp.dot(
                vext, p, preferred_element_type=jnp.float32)

    n_blocks = n_tiles + 1
    scores(0)
    for u in range(n_blocks):
        if u + 1 < n_blocks:
            scores(u + 1)
        accumulate(u)

    lam = _lambda(lamv_ref)
    for tile in range(2):
        o = (acc_ref[tile, 0, :V_DIM] / acc_ref[tile, 0, V_DIM:V_DIM + 1]
             - lam * (acc_ref[tile, 1, :V_DIM] / acc_ref[tile, 1, V_DIM:V_DIM + 1]))
        o = o * lax.rsqrt(jnp.mean(o * o, axis=0, keepdims=True) + SUBLN_EPS)
        o = o * gcol_ref[...] * (1.0 - LAM_INIT)
        o_ref[0, tile, 0] = o.T.astype(o_ref.dtype)


def _prompt_attn(qt, k, vt, lamv, gcol):
    b, s, d = k.shape
    tq, tk = ATTN_Q_TILE, ATTN_KV_TILE
    nq = s // tq
    assert tq == tk and s % tq == 0 and nq % 2 == 0
    return pl.pallas_call(
        functools.partial(_prompt_attn_body, n_tiles=nq),
        grid=(b, N_HEADS, nq // 2),
        in_specs=[
            pl.BlockSpec((1, V_DIM, tq), lambda bi, j, t: (bi * nq + t, j, 0)),
            pl.BlockSpec((1, V_DIM, tq), lambda bi, j, t: (bi * nq + nq - 1 - t, j, 0)),
            pl.BlockSpec((1, s, V_DIM), lambda bi, j, t: (bi, 0, j)),
            pl.BlockSpec((1, s // tk, V_DIM, tk), lambda bi, j, t: (bi, 0, j, 0)),
            _const_spec(lamv.shape), _const_spec(gcol.shape),
        ],
        out_specs=pl.BlockSpec((1, 2, 1, tq, V_DIM), lambda bi, j, t: (bi, 0, t, 0, j)),
        out_shape=jax.ShapeDtypeStruct((b, 2, nq // 2, tq, d), jnp.bfloat16),
        scratch_shapes=[
            pltpu.VMEM((2, 2, V_DIM, tq), jnp.bfloat16),
            pltpu.VMEM((2, tk, tq), jnp.float32),
            pltpu.VMEM((2, tk, tq), jnp.float32),
            pltpu.VMEM((2, 2, 1, tq), jnp.float32),
            pltpu.VMEM((2, 2, 1, tq), jnp.float32),
            pltpu.VMEM((2, 2, V_DIM + BF16_SUBLANES, tq), jnp.float32),
        ],
        compiler_params=_params(3),
        name="prompt_attn",
    )(qt, qt, k, vt, lamv, gcol)


def _paired_tile_slot(t, nq):
    return jnp.where(t < nq // 2, t, nq // 2 + nq - 1 - t)


class _DecodeRefs(NamedTuple):
    q: object
    kn: object
    vn: object
    lamv: object
    g: object
    k_pages: tuple
    v_pages: tuple
    out: object
    m: object
    l: object
    acc: object
    v3: object
    s: object
    alpha: object


def _decode_math(r):
    ds = r.q.shape[2]
    q3 = _bf16(r.q[0])

    def pair_rows(s):
        return s.reshape(N_HEADS, 2 * ds, s.shape[2])

    def scores(k3):
        return pair_rows(jnp.einsum("hqd,htd->hqt", q3, k3, preferred_element_type=jnp.float32))

    def scores_t(kt3):
        return pair_rows(jnp.einsum("hqd,hdt->hqt", q3, kt3, preferred_element_type=jnp.float32))

    def weighted(p, v3):
        return jnp.einsum("jrt,jte->jre", _bf16(p), v3, preferred_element_type=jnp.float32)

    return ds, scores, scores_t, weighted


def _decode_init(r):
    ds, scores, _, weighted = _decode_math(r)
    page = r.v_pages[0].shape[1]
    kn, vn = r.kn[0], r.vn[0]
    kn3 = _bf16(jnp.concatenate(
        [kn, jnp.zeros((kn.shape[0], page - ds, kn.shape[2]), jnp.float32)], axis=1))
    vn3 = _bf16(jnp.concatenate(
        [vn, jnp.zeros((vn.shape[0], page - ds, vn.shape[2]), jnp.float32)], axis=1))
    s = scores(kn3)
    qi = lax.broadcasted_iota(jnp.int32, s.shape, 1) % ds
    tt = lax.broadcasted_iota(jnp.int32, s.shape, 2)
    s = jnp.where(tt <= qi, s, jnp.finfo(jnp.float32).min)
    m = jnp.max(s, axis=-1, keepdims=True)
    p = jnp.exp(s - m)
    r.m[...] = m
    r.l[...] = jnp.sum(p, axis=-1, keepdims=True)
    r.acc[...] = weighted(p, vn3)


def _head_rows(tokens):
    return tokens // 2 + SUBLANES


def _decode_values(r):
    page = r.v_pages[0].shape[1]
    stride = _head_rows(page * len(r.v_pages))
    for n, ref in enumerate(r.v_pages):
        for t in range(0, page, 2):
            words = pltpu.pack_elementwise([ref[0, t], ref[0, t + 1]],
                                           packed_dtype=jnp.bfloat16)
            r.v3[pl.ds((n * page + t) // 2, N_HEADS, stride=stride), :] = words


def _decode_scores(r):
    _, _, scores_t, _ = _decode_math(r)
    r.s[...] = scores_t(jnp.concatenate([_bf16(ref[0]) for ref in r.k_pages], axis=2))


def _decode_softmax(r):
    s = r.s[...]
    m_old = r.m[...]
    m_new = jnp.maximum(m_old, jnp.max(s, axis=-1, keepdims=True))
    alpha = jnp.exp(m_old - m_new)
    p = jnp.exp(s - m_new)
    r.l[...] = alpha * r.l[...] + jnp.sum(p, axis=-1, keepdims=True)
    r.m[...] = m_new
    r.alpha[...] = alpha
    r.s[...] = p


def _decode_accumulate(r):
    _, _, _, weighted = _decode_math(r)
    tokens = r.s.shape[2]
    stride = _head_rows(tokens)
    v3 = jnp.stack([pltpu.bitcast(r.v3[j * stride:j * stride + tokens // 2, :], jnp.bfloat16)
                    for j in range(N_HEADS)])
    r.acc[...] = r.alpha[...] * r.acc[...] + weighted(r.s[...], v3)


def _decode_finish(r):
    ds = r.q.shape[2]
    o = r.acc[...] / r.l[...]
    o = o[:, :ds, :] - _lambda(r.lamv) * o[:, ds:, :]
    o = _subln(o, r.g[...])
    for j in range(N_HEADS):
        r.out[0, :, j * V_DIM:(j + 1) * V_DIM] = o[j]


def _ffn_decode_body(pt_ref, x_ref, w_in_ref, w_out_ref, g_ref, b_ref,
                     q_ref, kn_ref, vn_ref, lamv_ref, gsub_ref, *refs, chunks_per_seq):
    pps = PAGES_PER_STEP
    o_ref, od_ref, y_ref = refs[2 * pps:2 * pps + 3]
    dec = _DecodeRefs(q_ref, kn_ref, vn_ref, lamv_ref, gsub_ref,
                      refs[:pps], refs[pps:2 * pps], od_ref, *refs[2 * pps + 3:])
    s = pl.program_id(0)
    half = s % 2
    chunk = s % chunks_per_seq

    @pl.when(chunk == 0)
    def _():
        _decode_init(dec)

    def step(h_static):
        x = x_ref[...]
        _decode_values(dec)
        _decode_scores(dec)
        y = _ffn_part(x, w_in_ref, w_out_ref, *_ffn_split(w_out_ref.shape[1])[h_static])
        _decode_softmax(dec)
        _decode_accumulate(dec)
        if h_static == 0:
            y_ref[...] = y
        else:
            o_ref[...] = _layernorm(ALPHA * x + (y_ref[...] + y), g_ref[...], b_ref[...])

    for h_static in range(2):
        pl.when(half == h_static)(functools.partial(step, h_static))

    @pl.when(chunk == chunks_per_seq - 1)
    def _():
        _decode_finish(dec)


def _ffn_decode(x, w_in, w_out, layer, g, b,
                q3, k_new3, v_new3, cache_kt, cache_v, page_table, lamv, gsub):
    n, d = x.shape
    sb, nh2, ds, hd = q3.shape
    n_pages = page_table.shape[1]
    page = cache_v.shape[1]
    pps = PAGES_PER_STEP
    chunks = n_pages // pps
    rows = SUB_TILE
    n_steps = 2 * (n // rows)
    assert n_pages % pps == 0 and ds == SUBLANES and n_steps == sb * chunks

    def page_spec(shape, pg):
        return pl.BlockSpec(
            (1,) + shape[1:],
            lambda s, pt: (pt[s // chunks * n_pages + s % chunks * pps + pg], 0, 0, 0))

    def seq_spec(shape):
        return pl.BlockSpec((1,) + shape[1:],
                            lambda s, pt: (s // chunks,) + (0,) * (len(shape) - 1))

    row_spec = pl.BlockSpec((rows, d), lambda s, pt: (s // 2, 0))
    grid_spec = pltpu.PrefetchScalarGridSpec(
        num_scalar_prefetch=1,
        grid=(n_steps,),
        in_specs=[row_spec, _layer_spec(w_in.shape, layer), _layer_spec(w_out.shape, layer),
                  _const_spec(g.shape), _const_spec(b.shape),
                  seq_spec(q3.shape), seq_spec(k_new3.shape), seq_spec(v_new3.shape),
                  _const_spec(lamv.shape), _const_spec(gsub.shape)]
                 + [page_spec(cache_kt.shape, pg) for pg in range(pps)]
                 + [page_spec(cache_v.shape, pg) for pg in range(pps)],
        out_specs=[row_spec, seq_spec((sb, ds, d))],
        scratch_shapes=[
            pltpu.VMEM((rows, d), jnp.float32),
            pltpu.VMEM((N_HEADS, 2 * ds, 1), jnp.float32),
            pltpu.VMEM((N_HEADS, 2 * ds, 1), jnp.float32),
            pltpu.VMEM((N_HEADS, 2 * ds, V_DIM), jnp.float32),
            pltpu.VMEM((N_HEADS * _head_rows(pps * page), V_DIM), jnp.uint32),
            pltpu.VMEM((N_HEADS, 2 * ds, pps * page), jnp.float32),
            pltpu.VMEM((N_HEADS, 2 * ds, 1), jnp.float32),
        ],
    )
    return pl.pallas_call(
        functools.partial(_ffn_decode_body, chunks_per_seq=chunks),
        grid_spec=grid_spec,
        out_shape=[jax.ShapeDtypeStruct((n, d), jnp.float32),
                   jax.ShapeDtypeStruct((sb, ds, d), jnp.float32)],
        compiler_params=_params(1),
        name="ffn_decode",
    )(page_table.reshape(-1), x, w_in, w_out, g, b, q3, k_new3, v_new3, lamv, gsub,
      *([cache_kt] * pps), *([cache_v] * pps))


def _proj_ln_body(*refs):
    o_refs, (x_ref, w_ref, g_ref, b_ref, y_ref) = refs[:-5], refs[-5:]
    to = o_refs[0].shape[0]
    for n, o_ref in enumerate(o_refs):
        for sub in _sub_tiles(to):
            rows = slice(n * to + sub.start, n * to + sub.stop)
            y = jnp.dot(_bf16(o_ref[sub, :]), w_ref[...], preferred_element_type=jnp.float32)
            y_ref[rows, :] = _layernorm(ALPHA * x_ref[rows, :] + y, g_ref[...], b_ref[...])


def _proj_ln(o, x, w, g, b, tm, to, o_tile=lambda i: i):
    n, d = x.shape
    assert n % tm == 0 and tm % to == 0
    per = tm // to
    row = pl.BlockSpec((tm, d), lambda i: (i, 0))
    o_specs = [pl.BlockSpec((to, d), lambda i, k=k: (o_tile(i * per + k), 0)) for k in range(per)]
    return pl.pallas_call(
        _proj_ln_body,
        grid=(n // tm,),
        in_specs=o_specs + [row, _const_spec(w.shape), _const_spec(g.shape), _const_spec(b.shape)],
        out_specs=row,
        out_shape=jax.ShapeDtypeStruct((n, d), jnp.float32),
        compiler_params=_params(1),
        name="proj_ln",
    )(*([o] * per), x, w, g, b)


def kernel(x_prompt, x_sample, state_conv, cache_k, cache_v, page_table, w_conv_in, w_conv, w_conv_out, w_qkv, lambda_q1, lambda_k1, lambda_q2, lambda_k2, subln_g, w_attn_out, ln_mix_g, ln_mix_b, w_ffn_in, w_ffn_out, ln_ffn_g, ln_ffn_b):
    b, s, d = x_prompt.shape
    db, ds, _ = x_sample.shape
    n_pool, page = cache_k.shape[:2]
    past_len = page_table.shape[1] * page
    f32 = jnp.float32
    assert ds == SUBLANES and s % TOKEN_TILE == 0

    w_conv_in_b, w_conv_out_b = _bf16(w_conv_in), _bf16(w_conv_out)
    w_qkv_b, w_attn_out_b = _bf16(w_qkv), _bf16(w_attn_out)
    w_ffn_in_b, w_ffn_out_b = _bf16(w_ffn_in), _bf16(w_ffn_out)
    taps = w_conv.astype(f32)
    row = lambda a: a.reshape(1, -1).astype(f32)
    lamv = jnp.stack([lambda_q1, lambda_k1, lambda_q2, lambda_k2]).astype(f32)
    g_sub = row(subln_g)

    xp = x_prompt.reshape(b * s, d)
    xs = x_sample.reshape(db * ds, d)

    i = 0
    xp, tail_p = _conv_prompt(xp, w_conv_in_b, taps, w_conv_out_b,
                              row(ln_mix_g[i]), row(ln_mix_b[i]), s)
    conv_p = tail_p.reshape(b, SUBLANES, d)[:, SUBLANES - (CONV_WIDTH - 1):]
    st = jnp.pad(state_conv, ((0, 0), (0, ds - (CONV_WIDTH - 1)), (0, 0))).reshape(db * ds, d)
    xs, u_s = _conv_sample(xs, st, w_conv_in_b, taps, w_conv_out_b,
                           row(ln_mix_g[i]), row(ln_mix_b[i]))
    conv_s = u_s.reshape(db, ds, d)[:, ds - (CONV_WIDTH - 1):]
    xs = _ffn_sample(xs, w_ffn_in_b, w_ffn_out_b, i, row(ln_ffn_g[i]), row(ln_ffn_b[i]))

    tab_s = _rope_tables(past_len + jnp.arange(db * ds) % ds)
    qs, k_s, v_s = _qkv_sample(xs, w_qkv_b, tab_s)
    heads_first = lambda a, nh: a.reshape(db, ds, nh, d // nh).transpose(0, 2, 1, 3)
    q3, k_new3, v_new3 = (heads_first(qs, 2 * N_HEADS), heads_first(k_s, 2 * N_HEADS),
                          heads_first(v_s, N_HEADS))
    cache_kt = cache_k.transpose(0, 2, 3, 1)
    hb = db // 2

    def ffn_decode(x, layer, seqs):
        return _ffn_decode(x, w_ffn_in_b, w_ffn_out_b, layer,
                           row(ln_ffn_g[layer]), row(ln_ffn_b[layer]),
                           q3[seqs], k_new3[seqs], v_new3[seqs], cache_kt, cache_v,
                           page_table[seqs], lamv, g_sub)

    xp, os_lo = ffn_decode(xp, i, slice(0, hb))

    i = 1
    tab_p = _rope_tables(jnp.arange(s))
    qt_p, kb_p, kt_p, vt_p, v_p = _qkv_prompt(xp, w_qkv_b, tab_p, s)
    k_p = kt_p.reshape(b, 2 * N_HEADS, HEAD_DIM, s).transpose(0, 3, 1, 2)

    op = _prompt_attn(qt_p, kb_p.reshape(b, s, d),
                      vt_p.reshape(b, s // ATTN_KV_TILE, d, ATTN_KV_TILE),
                      lamv, g_sub.reshape(-1, 1))
    nq = s // ATTN_Q_TILE
    xp = _proj_ln(op.reshape(b * s, d), xp, w_attn_out_b, row(ln_mix_g[i]), row(ln_mix_b[i]),
                  tm=TOKEN_TILE, to=ATTN_Q_TILE,
                  o_tile=lambda r: r // nq * nq + _paired_tile_slot(r % nq, nq))
    xp, os_hi = ffn_decode(xp, i, slice(hb, db))
    os_ = jnp.concatenate([os_lo, os_hi], axis=0)
    xs = _proj_ln(os_.reshape(db * ds, d), xs, w_attn_out_b, row(ln_mix_g[i]), row(ln_mix_b[i]),
                  tm=db * ds, to=db * ds)
    xs = _ffn_sample(xs, w_ffn_in_b, w_ffn_out_b, i, row(ln_ffn_g[i]), row(ln_ffn_b[i]))

    return (xp.reshape(b, s, d), xs.reshape(db, ds, d), conv_p,
            k_p, v_p.reshape(b, s, N_HEADS, V_DIM),
            conv_s,
            k_s.reshape(db, ds, 2 * N_HEADS, HEAD_DIM), v_s.reshape(db, ds, N_HEADS, V_DIM))
```

```python
import functools
import math
from typing import NamedTuple

import jax
import jax.numpy as jnp
from jax import lax
from jax.experimental import pallas as pl
from jax.experimental.pallas import tpu as pltpu

N_HEADS = 8
HEAD_DIM = 64
V_DIM = 2 * HEAD_DIM
ROT_DIM = HEAD_DIM // 4
ROPE_THETA = 500000.0
CONV_WIDTH = 3
DEPTH = 2
LN_EPS = 1e-5
SUBLN_EPS = 1e-5
ALPHA = (2 * DEPTH) ** 0.25
SCALE = HEAD_DIM ** -0.5
LOG2_E = math.log2(math.e)
ATTN_LAYER = 1
LAM_INIT = 0.8 - 0.6 * math.exp(-0.3 * ATTN_LAYER)

LANES = 128
SUBLANES = 8
BF16_SUBLANES = 16
MXU_DIM = 256
VMEM_LIMIT_BYTES = 56 * 1024 * 1024

TOKEN_TILE = 1024
SUB_TILE = 256
ATTN_Q_TILE = 512
ATTN_KV_TILE = 512
PAGES_PER_STEP = 8

def _bf16(x):
    return x.astype(jnp.bfloat16)


def _layernorm(y, g, b):
    mu = jnp.mean(y, axis=-1, keepdims=True)
    yc = y - mu
    var = jnp.mean(yc * yc, axis=-1, keepdims=True)
    return yc * lax.rsqrt(var + LN_EPS) * g + b


def _const_spec(shape):
    nd = len(shape)
    return pl.BlockSpec(shape, lambda *_: (0,) * nd, pipeline_mode=pl.Buffered(1))


def _params(n_axes):
    return pltpu.CompilerParams(
        dimension_semantics=("arbitrary",) * n_axes,
        vmem_limit_bytes=VMEM_LIMIT_BYTES)


def _sub_tiles(rows):
    step = min(rows, SUB_TILE)
    return [slice(r, r + step) for r in range(0, rows, step)]


def _conv_prompt_body(x_ref, w_in_ref, taps_ref, w_out_ref, g_ref, b_ref,
                      o_ref, tail_ref, carry_ref, *, tiles_per_seq):
    i = pl.program_id(0)
    d = x_ref.shape[1]

    @pl.when(i % tiles_per_seq == 0)
    def _():
        carry_ref[...] = jnp.zeros_like(carry_ref)

    x = x_ref[...]
    t = x.shape[0]
    h3 = jnp.dot(_bf16(x), w_in_ref[...], preferred_element_type=jnp.float32)
    gb, gc, h = h3[:, :d], h3[:, d:2 * d], h3[:, 2 * d:]
    u = gc * h
    row = lax.broadcasted_iota(jnp.int32, (t, 1), 0)
    c6 = carry_ref[SUBLANES - 2:SUBLANES - 1, :]
    c7 = carry_ref[SUBLANES - 1:SUBLANES, :]
    u1 = jnp.where(row == 0, c7, pltpu.roll(u, 1, 0))
    u2 = jnp.where(row == 0, c6, jnp.where(row == 1, c7, pltpu.roll(u, 2, 0)))
    taps = taps_ref[...]
    conv = taps[0:1, :] * u2 + taps[1:2, :] * u1 + taps[2:3, :] * u
    y = jnp.dot(_bf16(gb * conv), w_out_ref[...], preferred_element_type=jnp.float32)
    o_ref[...] = _layernorm(ALPHA * x + y, g_ref[...], b_ref[...])

    carry_ref[...] = u[t - SUBLANES:, :]

    @pl.when(i % tiles_per_seq == tiles_per_seq - 1)
    def _():
        tail_ref[...] = u[t - SUBLANES:, :]


def _conv_prompt(x, w_in, taps, w_out, g, b, seq_len):
    n, d = x.shape
    tm = TOKEN_TILE
    tiles_per_seq = seq_len // tm
    return pl.pallas_call(
        functools.partial(_conv_prompt_body, tiles_per_seq=tiles_per_seq),
        grid=(n // tm,),
        in_specs=[
            pl.BlockSpec((tm, d), lambda i: (i, 0)),
            _const_spec(w_in.shape), _const_spec(taps.shape), _const_spec(w_out.shape),
            _const_spec(g.shape), _const_spec(b.shape),
        ],
        out_specs=[
            pl.BlockSpec((tm, d), lambda i: (i, 0)),
            pl.BlockSpec((SUBLANES, d), lambda i: (i // tiles_per_seq, 0)),
        ],
        out_shape=[
            jax.ShapeDtypeStruct((n, d), jnp.float32),
            jax.ShapeDtypeStruct((n // seq_len * SUBLANES, d), jnp.float32),
        ],
        scratch_shapes=[pltpu.VMEM((SUBLANES, d), jnp.float32)],
        compiler_params=_params(1),
        name="conv_prompt",
    )(x, w_in, taps, w_out, g, b)


def _conv_sample_body(x_ref, st_ref, w_in_ref, taps_ref, w_out_ref, g_ref, b_ref,
                      o_ref, u_ref):
    x = x_ref[...]
    t, d = x.shape
    h3 = jnp.dot(_bf16(x), w_in_ref[...], preferred_element_type=jnp.float32)
    gb, gc, h = h3[:, :d], h3[:, d:2 * d], h3[:, 2 * d:]
    u = gc * h
    st = st_ref[...]
    pos = lax.broadcasted_iota(jnp.int32, (t, 1), 0) % SUBLANES
    u1 = jnp.where(pos == 0, pltpu.roll(st, t - 1, 0), pltpu.roll(u, 1, 0))
    u2 = jnp.where(pos < 2, st, pltpu.roll(u, 2, 0))
    taps = taps_ref[...]
    conv = taps[0:1, :] * u2 + taps[1:2, :] * u1 + taps[2:3, :] * u
    y = jnp.dot(_bf16(gb * conv), w_out_ref[...], preferred_element_type=jnp.float32)
    o_ref[...] = _layernorm(ALPHA * x + y, g_ref[...], b_ref[...])
    u_ref[...] = u


def _conv_sample(x, st, w_in, taps, w_out, g, b):
    n, d = x.shape
    return pl.pallas_call(
        _conv_sample_body,
        grid=(1,),
        in_specs=[_const_spec(a.shape) for a in (x, st, w_in, taps, w_out, g, b)],
        out_specs=[_const_spec((n, d)), _const_spec((n, d))],
        out_shape=[jax.ShapeDtypeStruct((n, d), jnp.float32)] * 2,
        compiler_params=_params(1),
        name="conv_sample",
    )(x, st, w_in, taps, w_out, g, b)


def _ffn_split(f):
    assert f % MXU_DIM == 0
    cut = (f // MXU_DIM + 1) // 2 * MXU_DIM
    return (0, cut), (cut, f)


def _ffn_part(x, w_in_ref, w_out_ref, lo, hi):
    f = w_out_ref.shape[1]
    xb = _bf16(x)
    gate = jnp.dot(xb, w_in_ref[0, :, lo:hi], preferred_element_type=jnp.float32)
    up = jnp.dot(xb, w_in_ref[0, :, f + lo:f + hi], preferred_element_type=jnp.float32)
    a = gate * jax.nn.sigmoid(gate) * up
    return jnp.dot(_bf16(a), w_out_ref[0, lo:hi, :], preferred_element_type=jnp.float32)


def _ffn_sample_body(x_ref, w_in_ref, w_out_ref, g_ref, b_ref, o_ref):
    x = x_ref[...]
    y = _ffn_part(x, w_in_ref, w_out_ref, 0, w_out_ref.shape[1])
    o_ref[...] = _layernorm(ALPHA * x + y, g_ref[...], b_ref[...])


def _layer_spec(shape, layer):
    nd = len(shape)
    return pl.BlockSpec((1,) + shape[1:], lambda *_: (layer,) + (0,) * (nd - 1),
                        pipeline_mode=pl.Buffered(1))


def _ffn_sample(x, w_in, w_out, layer, g, b):
    n, d = x.shape
    return pl.pallas_call(
        _ffn_sample_body,
        grid=(1,),
        in_specs=[_const_spec(x.shape), _layer_spec(w_in.shape, layer),
                  _layer_spec(w_out.shape, layer), _const_spec(g.shape), _const_spec(b.shape)],
        out_specs=_const_spec((n, d)),
        out_shape=jax.ShapeDtypeStruct((n, d), jnp.float32),
        compiler_params=_params(1),
        name="ffn_sample",
    )(x, w_in, w_out, g, b)


def _rope_tables(pos):
    half = ROT_DIM // 2
    inv = jnp.power(ROPE_THETA, -jnp.arange(0, ROT_DIM, 2, dtype=jnp.float32) / ROT_DIM)
    ang = pos.astype(jnp.float32)[:, None] * inv[None, :]
    cos, sin = jnp.cos(ang), jnp.sin(ang)
    dd = jnp.arange(LANES) % HEAD_DIM
    cos_l = jnp.take(cos, dd % half, axis=1)
    sin_l = jnp.take(sin, dd % half, axis=1)
    c = jnp.where(dd[None, :] < ROT_DIM, cos_l, 1.0)
    s_up = jnp.where(dd[None, :] < half, -sin_l, 0.0)
    s_dn = jnp.where((dd[None, :] >= half) & (dd[None, :] < ROT_DIM), sin_l, 0.0)
    return c, s_up, s_dn


def _rope(x, c, s_up, s_dn):
    half = ROT_DIM // 2
    outs = []
    for g in range(x.shape[1] // LANES):
        xg = x[:, g * LANES:(g + 1) * LANES]
        x_up = pltpu.roll(xg, LANES - half, 1)
        x_dn = pltpu.roll(xg, half, 1)
        outs.append(xg * c + x_up * s_up + x_dn * s_dn)
    return jnp.concatenate(outs, axis=1)


def _qkv_rows(x_ref, w_ref, c_ref, su_ref, sd_ref, rows=slice(None)):
    x = x_ref[rows, :]
    d = x.shape[1]
    h3 = jnp.dot(_bf16(x), w_ref[...], preferred_element_type=jnp.float32)
    c, su, sd = c_ref[rows, :], su_ref[rows, :], sd_ref[rows, :]
    q = _rope(h3[:, :d], c, su, sd)
    k = _rope(h3[:, d:2 * d], c, su, sd)
    return q, k, h3[:, 2 * d:]


def _qkv_sample_body(x_ref, w_ref, c_ref, su_ref, sd_ref, q_ref, k_ref, v_ref):
    q, k, v = _qkv_rows(x_ref, w_ref, c_ref, su_ref, sd_ref)
    q_ref[...] = q * SCALE
    k_ref[...] = k
    v_ref[...] = v


def _qkv_sample(x, w, tables):
    n, d = x.shape
    specs = [_const_spec(a.shape) for a in (x, w) + tuple(tables)]
    return pl.pallas_call(
        _qkv_sample_body,
        grid=(1,),
        in_specs=specs,
        out_specs=[_const_spec((n, d))] * 3,
        out_shape=[jax.ShapeDtypeStruct((n, d), jnp.float32)] * 3,
        compiler_params=_params(1),
        name="qkv_sample",
    )(x, w, *tables)


def _qkv_prompt_body(x_ref, w_ref, c_ref, su_ref, sd_ref,
                     qt_ref, kb_ref, kt_ref, vt_ref, v_ref):
    ta = qt_ref.shape[2]
    for c in range(qt_ref.shape[0]):
        for sub in _sub_tiles(ta):
            rows = slice(c * ta + sub.start, c * ta + sub.stop)
            q, k, v = _qkv_rows(x_ref, w_ref, c_ref, su_ref, sd_ref, rows)
            qt_ref[c, :, sub] = _bf16((q * (SCALE * LOG2_E)).T)
            kb_ref[rows, :] = _bf16(k)
            kt_ref[0, :, rows] = k.T
            vt_ref[c, :, sub] = _bf16(v.T)
            v_ref[rows, :] = v


def _qkv_prompt(x, w, tables, seq_len):
    n, d = x.shape
    tm = TOKEN_TILE
    ta = ATTN_Q_TILE
    assert ATTN_Q_TILE == ATTN_KV_TILE and tm % ta == 0
    tps = seq_len // tm
    tspec = pl.BlockSpec((tm, LANES), lambda i: (i % tps, 0))
    row = pl.BlockSpec((tm, d), lambda i: (i, 0))
    blk = pl.BlockSpec((tm // ta, d, ta), lambda i: (i, 0, 0))
    return pl.pallas_call(
        _qkv_prompt_body,
        grid=(n // tm,),
        in_specs=[row, _const_spec(w.shape), tspec, tspec, tspec],
        out_specs=[blk, row, pl.BlockSpec((1, d, tm), lambda i: (i // tps, 0, i % tps)), blk, row],
        out_shape=[
            jax.ShapeDtypeStruct((n // ta, d, ta), jnp.bfloat16),
            jax.ShapeDtypeStruct((n, d), jnp.bfloat16),
            jax.ShapeDtypeStruct((n // seq_len, d, seq_len), jnp.float32),
            jax.ShapeDtypeStruct((n // ta, d, ta), jnp.bfloat16),
            jax.ShapeDtypeStruct((n, d), jnp.float32),
        ],
        compiler_params=_params(1),
        name="qkv_prompt",
    )(x, w, *tables)


def _lambda(lamv_ref):
    lv = lamv_ref[...]
    d1 = jnp.sum(lv[0:1, :] * lv[1:2, :], axis=-1, keepdims=True)
    d2 = jnp.sum(lv[2:3, :] * lv[3:4, :], axis=-1, keepdims=True)
    return jnp.exp(d1) - jnp.exp(d2) + LAM_INIT


def _subln(o, g):
    o = o * lax.rsqrt(jnp.mean(o * o, axis=-1, keepdims=True) + SUBLN_EPS)
    return o * g * (1.0 - LAM_INIT)


def _prompt_attn_body(qlo_ref, qhi_ref, k_ref, vt_ref, lamv_ref, gcol_ref, o_ref,
                      q_ref, s0_ref, s1_ref, bmax_ref, m_ref, acc_ref, *, n_tiles):
    tq, tk = ATTN_Q_TILE, ATTN_KV_TILE
    t_lo = pl.program_id(2)
    t_hi = n_tiles - 1 - t_lo
    neg = jnp.finfo(jnp.float32).min
    feat = lax.broadcasted_iota(jnp.int32, (2 * HEAD_DIM, 1), 0)
    for tile, ref in enumerate((qlo_ref, qhi_ref)):
        q_ref[tile, 0] = jnp.where(feat < HEAD_DIM, ref[0], 0)
        q_ref[tile, 1] = jnp.where(feat >= HEAD_DIM, ref[0], 0)
    ones = jnp.ones((acc_ref.shape[2] - V_DIM, tk), jnp.bfloat16)
    acc_ref[...] = jnp.zeros_like(acc_ref)
    m_ref[...] = jnp.full_like(m_ref, neg)
    s_refs = (s0_ref, s1_ref)

    def block(u):
        if u == 0:
            return 0, t_lo, True
        if u == 1:
            return 1, t_hi, True
        n = u - 2
        return jnp.where(n < t_lo, 0, 1), jnp.where(n < t_lo, n, n - t_lo), False

    def scores(u):
        tile, kv, masked = block(u)
        kblk = k_ref[0, pl.ds(pl.multiple_of(kv * tk, tk), tk), :]
        for c in range(2):
            st = jnp.dot(kblk, q_ref[tile, c], preferred_element_type=jnp.float32)
            if masked:
                key = lax.broadcasted_iota(jnp.int32, (tk, tq), 0)
                qry = lax.broadcasted_iota(jnp.int32, (tk, tq), 1)
                st = jnp.where(key <= qry, st, neg)
            s_refs[u % 2][c] = st
            bmax_ref[u % 2, c] = jnp.max(st, axis=0, keepdims=True)

    def accumulate(u):
        tile, kv, _ = block(u)
        s_ref = s_refs[u % 2]
        vext = jnp.concatenate([vt_ref[0, kv], ones], axis=0)
        for c in range(2):
            m_old = m_ref[tile, c]
            m_new = jnp.maximum(m_old, bmax_ref[u % 2, c])
            alpha = jnp.exp2(m_old - m_new)
            m_ref[tile, c] = m_new
            p = _bf16(jnp.exp2(s_ref[c] - m_new))
            acc_ref[tile, c] = alpha * acc_ref[tile, c] + jnp.dot(
                vext, p, preferred_element_type=jnp.float32)

    n_blocks = n_tiles + 1
    scores(0)
    for u in range(n_blocks):
        if u + 1 < n_blocks:
            scores(u + 1)
        accumulate(u)

    lam = _lambda(lamv_ref)
    for tile in range(2):
        o = (acc_ref[tile, 0, :V_DIM] / acc_ref[tile, 0, V_DIM:V_DIM + 1]
             - lam * (acc_ref[tile, 1, :V_DIM] / acc_ref[tile, 1, V_DIM:V_DIM + 1]))
        o = o * lax.rsqrt(jnp.mean(o * o, axis=0, keepdims=True) + SUBLN_EPS)
        o = o * gcol_ref[...] * (1.0 - LAM_INIT)
        o_ref[0, tile, 0] = o.T.astype(o_ref.dtype)


def _prompt_attn(qt, k, vt, lamv, gcol):
    b, s, d = k.shape
    tq, tk = ATTN_Q_TILE, ATTN_KV_TILE
    nq = s // tq
    assert tq == tk and s % tq == 0 and nq % 2 == 0
    return pl.pallas_call(
        functools.partial(_prompt_attn_body, n_tiles=nq),
        grid=(b, N_HEADS, nq // 2),
        in_specs=[
            pl.BlockSpec((1, V_DIM, tq), lambda bi, j, t: (bi * nq + t, j, 0)),
            pl.BlockSpec((1, V_DIM, tq), lambda bi, j, t: (bi * nq + nq - 1 - t, j, 0)),
            pl.BlockSpec((1, s, V_DIM), lambda bi, j, t: (bi, 0, j)),
            pl.BlockSpec((1, s // tk, V_DIM, tk), lambda bi, j, t: (bi, 0, j, 0)),
            _const_spec(lamv.shape), _const_spec(gcol.shape),
        ],
        out_specs=pl.BlockSpec((1, 2, 1, tq, V_DIM), lambda bi, j, t: (bi, 0, t, 0, j)),
        out_shape=jax.ShapeDtypeStruct((b, 2, nq // 2, tq, d), jnp.bfloat16),
        scratch_shapes=[
            pltpu.VMEM((2, 2, V_DIM, tq), jnp.bfloat16),
            pltpu.VMEM((2, tk, tq), jnp.float32),
            pltpu.VMEM((2, tk, tq), jnp.float32),
            pltpu.VMEM((2, 2, 1, tq), jnp.float32),
            pltpu.VMEM((2, 2, 1, tq), jnp.float32),
            pltpu.VMEM((2, 2, V_DIM + BF16_SUBLANES, tq), jnp.float32),
        ],
        compiler_params=_params(3),
        name="prompt_attn",
    )(qt, qt, k, vt, lamv, gcol)


def _paired_tile_slot(t, nq):
    return jnp.where(t < nq // 2, t, nq // 2 + nq - 1 - t)


class _DecodeRefs(NamedTuple):
    q: object
    kn: object
    vn: object
    lamv: object
    g: object
    k_pages: tuple
    v_pages: tuple
    out: object
    m: object
    l: object
    acc: object
    v3: object
    s: object
    alpha: object


def _decode_math(r):
    ds = r.q.shape[2]
    q3 = _bf16(r.q[0])

    def pair_rows(s):
        return s.reshape(N_HEADS, 2 * ds, s.shape[2])

    def scores(k3):
        return pair_rows(jnp.einsum("hqd,htd->hqt", q3, k3, preferred_element_type=jnp.float32))

    def scores_t(kt3):
        return pair_rows(jnp.einsum("hqd,hdt->hqt", q3, kt3, preferred_element_type=jnp.float32))

    def weighted(p, v3):
        return jnp.einsum("jrt,jte->jre", _bf16(p), v3, preferred_element_type=jnp.float32)

    return ds, scores, scores_t, weighted


def _decode_init(r):
    ds, scores, _, weighted = _decode_math(r)
    page = r.v_pages[0].shape[1]
    kn, vn = r.kn[0], r.vn[0]
    kn3 = _bf16(jnp.concatenate(
        [kn, jnp.zeros((kn.shape[0], page - ds, kn.shape[2]), jnp.float32)], axis=1))
    vn3 = _bf16(jnp.concatenate(
        [vn, jnp.zeros((vn.shape[0], page - ds, vn.shape[2]), jnp.float32)], axis=1))
    s = scores(kn3)
    qi = lax.broadcasted_iota(jnp.int32, s.shape, 1) % ds
    tt = lax.broadcasted_iota(jnp.int32, s.shape, 2)
    s = jnp.where(tt <= qi, s, jnp.finfo(jnp.float32).min)
    m = jnp.max(s, axis=-1, keepdims=True)
    p = jnp.exp(s - m)
    r.m[...] = m
    r.l[...] = jnp.sum(p, axis=-1, keepdims=True)
    r.acc[...] = weighted(p, vn3)


def _head_rows(tokens):
    return tokens // 2 + SUBLANES


def _decode_values(r):
    page = r.v_pages[0].shape[1]
    stride = _head_rows(page * len(r.v_pages))
    for n, ref in enumerate(r.v_pages):
        for t in range(0, page, 2):
            words = pltpu.pack_elementwise([ref[0, t], ref[0, t + 1]],
                                           packed_dtype=jnp.bfloat16)
            r.v3[pl.ds((n * page + t) // 2, N_HEADS, stride=stride), :] = words


def _decode_scores(r):
    _, _, scores_t, _ = _decode_math(r)
    r.s[...] = scores_t(jnp.concatenate([_bf16(ref[0]) for ref in r.k_pages], axis=2))


def _decode_softmax(r):
    s = r.s[...]
    m_old = r.m[...]
    m_new = jnp.maximum(m_old, jnp.max(s, axis=-1, keepdims=True))
    alpha = jnp.exp(m_old - m_new)
    p = jnp.exp(s - m_new)
    r.l[...] = alpha * r.l[...] + jnp.sum(p, axis=-1, keepdims=True)
    r.m[...] = m_new
    r.alpha[...] = alpha
    r.s[...] = p


def _decode_accumulate(r):
    _, _, _, weighted = _decode_math(r)
    tokens = r.s.shape[2]
    stride = _head_rows(tokens)
    v3 = jnp.stack([pltpu.bitcast(r.v3[j * stride:j * stride + tokens // 2, :], jnp.bfloat16)
                    for j in range(N_HEADS)])
    r.acc[...] = r.alpha[...] * r.acc[...] + weighted(r.s[...], v3)


def _decode_finish(r):
    ds = r.q.shape[2]
    o = r.acc[...] / r.l[...]
    o = o[:, :ds, :] - _lambda(r.lamv) * o[:, ds:, :]
    o = _subln(o, r.g[...])
    for j in range(N_HEADS):
        r.out[0, :, j * V_DIM:(j + 1) * V_DIM] = o[j]


def _ffn_decode_body(pt_ref, x_ref, w_in_ref, w_out_ref, g_ref, b_ref,
                     q_ref, kn_ref, vn_ref, lamv_ref, gsub_ref, *refs, chunks_per_seq):
    pps = PAGES_PER_STEP
    o_ref, od_ref, y_ref = refs[2 * pps:2 * pps + 3]
    dec = _DecodeRefs(q_ref, kn_ref, vn_ref, lamv_ref, gsub_ref,
                      refs[:pps], refs[pps:2 * pps], od_ref, *refs[2 * pps + 3:])
    s = pl.program_id(0)
    half = s % 2
    chunk = s % chunks_per_seq

    @pl.when(chunk == 0)
    def _():
        _decode_init(dec)

    def step(h_static):
        x = x_ref[...]
        _decode_values(dec)
        _decode_scores(dec)
        y = _ffn_part(x, w_in_ref, w_out_ref, *_ffn_split(w_out_ref.shape[1])[h_static])
        _decode_softmax(dec)
        _decode_accumulate(dec)
        if h_static == 0:
            y_ref[...] = y
        else:
            o_ref[...] = _layernorm(ALPHA * x + (y_ref[...] + y), g_ref[...], b_ref[...])

    for h_static in range(2):
        pl.when(half == h_static)(functools.partial(step, h_static))

    @pl.when(chunk == chunks_per_seq - 1)
    def _():
        _decode_finish(dec)


def _ffn_decode(x, w_in, w_out, layer, g, b,
                q3, k_new3, v_new3, cache_kt, cache_v, page_table, lamv, gsub):
    n, d = x.shape
    sb, nh2, ds, hd = q3.shape
    n_pages = page_table.shape[1]
    page = cache_v.shape[1]
    pps = PAGES_PER_STEP
    chunks = n_pages // pps
    rows = SUB_TILE
    n_steps = 2 * (n // rows)
    assert n_pages % pps == 0 and ds == SUBLANES and n_steps == sb * chunks

    def page_spec(shape, pg):
        return pl.BlockSpec(
            (1,) + shape[1:],
            lambda s, pt: (pt[s // chunks * n_pages + s % chunks * pps + pg], 0, 0, 0))

    def seq_spec(shape):
        return pl.BlockSpec((1,) + shape[1:],
                            lambda s, pt: (s // chunks,) + (0,) * (len(shape) - 1))

    row_spec = pl.BlockSpec((rows, d), lambda s, pt: (s // 2, 0))
    grid_spec = pltpu.PrefetchScalarGridSpec(
        num_scalar_prefetch=1,
        grid=(n_steps,),
        in_specs=[row_spec, _layer_spec(w_in.shape, layer), _layer_spec(w_out.shape, layer),
                  _const_spec(g.shape), _const_spec(b.shape),
                  seq_spec(q3.shape), seq_spec(k_new3.shape), seq_spec(v_new3.shape),
                  _const_spec(lamv.shape), _const_spec(gsub.shape)]
                 + [page_spec(cache_kt.shape, pg) for pg in range(pps)]
                 + [page_spec(cache_v.shape, pg) for pg in range(pps)],
        out_specs=[row_spec, seq_spec((sb, ds, d))],
        scratch_shapes=[
            pltpu.VMEM((rows, d), jnp.float32),
            pltpu.VMEM((N_HEADS, 2 * ds, 1), jnp.float32),
            pltpu.VMEM((N_HEADS, 2 * ds, 1), jnp.float32),
            pltpu.VMEM((N_HEADS, 2 * ds, V_DIM), jnp.float32),
            pltpu.VMEM((N_HEADS * _head_rows(pps * page), V_DIM), jnp.uint32),
            pltpu.VMEM((N_HEADS, 2 * ds, pps * page), jnp.float32),
            pltpu.VMEM((N_HEADS, 2 * ds, 1), jnp.float32),
        ],
    )
    return pl.pallas_call(
        functools.partial(_ffn_decode_body, chunks_per_seq=chunks),
        grid_spec=grid_spec,
        out_shape=[jax.ShapeDtypeStruct((n, d), jnp.float32),
                   jax.ShapeDtypeStruct((sb, ds, d), jnp.float32)],
        compiler_params=_params(1),
        name="ffn_decode",
    )(page_table.reshape(-1), x, w_in, w_out, g, b, q3, k_new3, v_new3, lamv, gsub,
      *([cache_kt] * pps), *([cache_v] * pps))


def _proj_ln_body(*refs):
    o_refs, (x_ref, w_ref, g_ref, b_ref, y_ref) = refs[:-5], refs[-5:]
    to = o_refs[0].shape[0]
    for n, o_ref in enumerate(o_refs):
        for sub in _sub_tiles(to):
            rows = slice(n * to + sub.start, n * to + sub.stop)
            y = jnp.dot(_bf16(o_ref[sub, :]), w_ref[...], preferred_element_type=jnp.float32)
            y_ref[rows, :] = _layernorm(ALPHA * x_ref[rows, :] + y, g_ref[...], b_ref[...])


def _proj_ln(o, x, w, g, b, tm, to, o_tile=lambda i: i):
    n, d = x.shape
    assert n % tm == 0 and tm % to == 0
    per = tm // to
    row = pl.BlockSpec((tm, d), lambda i: (i, 0))
    o_specs = [pl.BlockSpec((to, d), lambda i, k=k: (o_tile(i * per + k), 0)) for k in range(per)]
    return pl.pallas_call(
        _proj_ln_body,
        grid=(n // tm,),
        in_specs=o_specs + [row, _const_spec(w.shape), _const_spec(g.shape), _const_spec(b.shape)],
        out_specs=row,
        out_shape=jax.ShapeDtypeStruct((n, d), jnp.float32),
        compiler_params=_params(1),
        name="proj_ln",
    )(*([o] * per), x, w, g, b)


def kernel(x_prompt, x_sample, state_conv, cache_k, cache_v, page_table, w_conv_in, w_conv, w_conv_out, w_qkv, lambda_q1, lambda_k1, lambda_q2, lambda_k2, subln_g, w_attn_out, ln_mix_g, ln_mix_b, w_ffn_in, w_ffn_out, ln_ffn_g, ln_ffn_b):
    b, s, d = x_prompt.shape
    db, ds, _ = x_sample.shape
    n_pool, page = cache_k.shape[:2]
    past_len = page_table.shape[1] * page
    f32 = jnp.float32
    assert ds == SUBLANES and s % TOKEN_TILE == 0

    w_conv_in_b, w_conv_out_b = _bf16(w_conv_in), _bf16(w_conv_out)
    w_qkv_b, w_attn_out_b = _bf16(w_qkv), _bf16(w_attn_out)
    w_ffn_in_b, w_ffn_out_b = _bf16(w_ffn_in), _bf16(w_ffn_out)
    taps = w_conv.astype(f32)
    row = lambda a: a.reshape(1, -1).astype(f32)
    lamv = jnp.stack([lambda_q1, lambda_k1, lambda_q2, lambda_k2]).astype(f32)
    g_sub = row(subln_g)

    xp = x_prompt.reshape(b * s, d)
    xs = x_sample.reshape(db * ds, d)

    i = 0
    xp, tail_p = _conv_prompt(xp, w_conv_in_b, taps, w_conv_out_b,
                              row(ln_mix_g[i]), row(ln_mix_b[i]), s)
    conv_p = tail_p.reshape(b, SUBLANES, d)[:, SUBLANES - (CONV_WIDTH - 1):]
    st = jnp.pad(state_conv, ((0, 0), (0, ds - (CONV_WIDTH - 1)), (0, 0))).reshape(db * ds, d)
    xs, u_s = _conv_sample(xs, st, w_conv_in_b, taps, w_conv_out_b,
                           row(ln_mix_g[i]), row(ln_mix_b[i]))
    conv_s = u_s.reshape(db, ds, d)[:, ds - (CONV_WIDTH - 1):]
    xs = _ffn_sample(xs, w_ffn_in_b, w_ffn_out_b, i, row(ln_ffn_g[i]), row(ln_ffn_b[i]))

    tab_s = _rope_tables(past_len + jnp.arange(db * ds) % ds)
    qs, k_s, v_s = _qkv_sample(xs, w_qkv_b, tab_s)
    heads_first = lambda a, nh: a.reshape(db, ds, nh, d // nh).transpose(0, 2, 1, 3)
    q3, k_new3, v_new3 = (heads_first(qs, 2 * N_HEADS), heads_first(k_s, 2 * N_HEADS),
                          heads_first(v_s, N_HEADS))
    cache_kt = cache_k.transpose(0, 2, 3, 1)
    hb = db // 2

    def ffn_decode(x, layer, seqs):
        return _ffn_decode(x, w_ffn_in_b, w_ffn_out_b, layer,
                           row(ln_ffn_g[layer]), row(ln_ffn_b[layer]),
                           q3[seqs], k_new3[seqs], v_new3[seqs], cache_kt, cache_v,
                           page_table[seqs], lamv, g_sub)

    xp, os_lo = ffn_decode(xp, i, slice(0, hb))

    i = 1
    tab_p = _rope_tables(jnp.arange(s))
    qt_p, kb_p, kt_p, vt_p, v_p = _qkv_prompt(xp, w_qkv_b, tab_p, s)
    k_p = kt_p.reshape(b, 2 * N_HEADS, HEAD_DIM, s).transpose(0, 3, 1, 2)

    op = _prompt_attn(qt_p, kb_p.reshape(b, s, d),
                      vt_p.reshape(b, s // ATTN_KV_TILE, d, ATTN_KV_TILE),
                      lamv, g_sub.reshape(-1, 1))
    nq = s // ATTN_Q_TILE
    xp = _proj_ln(op.reshape(b * s, d), xp, w_attn_out_b, row(ln_mix_g[i]), row(ln_mix_b[i]),
                  tm=TOKEN_TILE, to=ATTN_Q_TILE,
                  o_tile=lambda r: r // nq * nq + _paired_tile_slot(r % nq, nq))
    xp, os_hi = ffn_decode(xp, i, slice(hb, db))
    os_ = jnp.concatenate([os_lo, os_hi], axis=0)
    xs = _proj_ln(os_.reshape(db * ds, d), xs, w_attn_out_b, row(ln_mix_g[i]), row(ln_mix_b[i]),
                  tm=db * ds, to=db * ds)
    xs = _ffn_sample(xs, w_ffn_in_b, w_ffn_out_b, i, row(ln_ffn_g[i]), row(ln_ffn_b[i]))

    return (xp.reshape(b, s, d), xs.reshape(db, ds, d), conv_p,
            k_p, v_p.reshape(b, s, N_HEADS, V_DIM),
            conv_s,
            k_s.reshape(db, ds, 2 * N_HEADS, HEAD_DIM), v_s.reshape(db, ds, N_HEADS, V_DIM))
```

```python
import functools
import math
from typing import NamedTuple

import jax
import jax.numpy as jnp
from jax import lax
from jax.experimental import pallas as pl
from jax.experimental.pallas import tpu as pltpu

N_HEADS = 8
HEAD_DIM = 64
V_DIM = 2 * HEAD_DIM
ROT_DIM = HEAD_DIM // 4
ROPE_THETA = 500000.0
CONV_WIDTH = 3
DEPTH = 2
LN_EPS = 1e-5
SUBLN_EPS = 1e-5
ALPHA = (2 * DEPTH) ** 0.25
SCALE = HEAD_DIM ** -0.5
LOG2_E = math.log2(math.e)
ATTN_LAYER = 1
LAM_INIT = 0.8 - 0.6 * math.exp(-0.3 * ATTN_LAYER)

LANES = 128
SUBLANES = 8
BF16_SUBLANES = 16
MXU_DIM = 256
VMEM_LIMIT_BYTES = 56 * 1024 * 1024

TOKEN_TILE = 1024
SUB_TILE = 256
ATTN_Q_TILE = 512
ATTN_KV_TILE = 512
PAGES_PER_STEP = 8
PAGE_SLOTS = 3

def _bf16(x):
    return x.astype(jnp.bfloat16)


def _layernorm(y, g, b):
    mu = jnp.mean(y, axis=-1, keepdims=True)
    yc = y - mu
    var = jnp.mean(yc * yc, axis=-1, keepdims=True)
    return yc * lax.rsqrt(var + LN_EPS) * g + b


def _const_spec(shape):
    nd = len(shape)
    return pl.BlockSpec(shape, lambda *_: (0,) * nd, pipeline_mode=pl.Buffered(1))


def _params(n_axes):
    return pltpu.CompilerParams(
        dimension_semantics=("arbitrary",) * n_axes,
        vmem_limit_bytes=VMEM_LIMIT_BYTES)


def _sub_tiles(rows):
    step = min(rows, SUB_TILE)
    return [slice(r, r + step) for r in range(0, rows, step)]


def _conv_prompt_body(x_ref, w_in_ref, taps_ref, w_out_ref, g_ref, b_ref,
                      o_ref, tail_ref, carry_ref, *, tiles_per_seq):
    i = pl.program_id(0)
    d = x_ref.shape[1]

    @pl.when(i % tiles_per_seq == 0)
    def _():
        carry_ref[...] = jnp.zeros_like(carry_ref)

    x = x_ref[...]
    t = x.shape[0]
    h3 = jnp.dot(_bf16(x), w_in_ref[...], preferred_element_type=jnp.float32)
    gb, gc, h = h3[:, :d], h3[:, d:2 * d], h3[:, 2 * d:]
    u = gc * h
    row = lax.broadcasted_iota(jnp.int32, (t, 1), 0)
    c6 = carry_ref[SUBLANES - 2:SUBLANES - 1, :]
    c7 = carry_ref[SUBLANES - 1:SUBLANES, :]
    u1 = jnp.where(row == 0, c7, pltpu.roll(u, 1, 0))
    u2 = jnp.where(row == 0, c6, jnp.where(row == 1, c7, pltpu.roll(u, 2, 0)))
    taps = taps_ref[...]
    conv = taps[0:1, :] * u2 + taps[1:2, :] * u1 + taps[2:3, :] * u
    y = jnp.dot(_bf16(gb * conv), w_out_ref[...], preferred_element_type=jnp.float32)
    o_ref[...] = _layernorm(ALPHA * x + y, g_ref[...], b_ref[...])

    carry_ref[...] = u[t - SUBLANES:, :]

    @pl.when(i % tiles_per_seq == tiles_per_seq - 1)
    def _():
        tail_ref[...] = u[t - SUBLANES:, :]


def _conv_prompt(x, w_in, taps, w_out, g, b, seq_len):
    n, d = x.shape
    tm = TOKEN_TILE
    tiles_per_seq = seq_len // tm
    return pl.pallas_call(
        functools.partial(_conv_prompt_body, tiles_per_seq=tiles_per_seq),
        grid=(n // tm,),
        in_specs=[
            pl.BlockSpec((tm, d), lambda i: (i, 0)),
            _const_spec(w_in.shape), _const_spec(taps.shape), _const_spec(w_out.shape),
            _const_spec(g.shape), _const_spec(b.shape),
        ],
        out_specs=[
            pl.BlockSpec((tm, d), lambda i: (i, 0)),
            pl.BlockSpec((SUBLANES, d), lambda i: (i // tiles_per_seq, 0)),
        ],
        out_shape=[
            jax.ShapeDtypeStruct((n, d), jnp.float32),
            jax.ShapeDtypeStruct((n // seq_len * SUBLANES, d), jnp.float32),
        ],
        scratch_shapes=[pltpu.VMEM((SUBLANES, d), jnp.float32)],
        compiler_params=_params(1),
        name="conv_prompt",
    )(x, w_in, taps, w_out, g, b)


def _conv_sample_body(x_ref, st_ref, w_in_ref, taps_ref, w_out_ref, g_ref, b_ref,
                      o_ref, u_ref):
    x = x_ref[...]
    t, d = x.shape
    h3 = jnp.dot(_bf16(x), w_in_ref[...], preferred_element_type=jnp.float32)
    gb, gc, h = h3[:, :d], h3[:, d:2 * d], h3[:, 2 * d:]
    u = gc * h
    st = st_ref[...]
    pos = lax.broadcasted_iota(jnp.int32, (t, 1), 0) % SUBLANES
    u1 = jnp.where(pos == 0, pltpu.roll(st, t - 1, 0), pltpu.roll(u, 1, 0))
    u2 = jnp.where(pos < 2, st, pltpu.roll(u, 2, 0))
    taps = taps_ref[...]
    conv = taps[0:1, :] * u2 + taps[1:2, :] * u1 + taps[2:3, :] * u
    y = jnp.dot(_bf16(gb * conv), w_out_ref[...], preferred_element_type=jnp.float32)
    o_ref[...] = _layernorm(ALPHA * x + y, g_ref[...], b_ref[...])
    u_ref[...] = u


def _conv_sample(x, st, w_in, taps, w_out, g, b):
    n, d = x.shape
    return pl.pallas_call(
        _conv_sample_body,
        grid=(1,),
        in_specs=[_const_spec(a.shape) for a in (x, st, w_in, taps, w_out, g, b)],
        out_specs=[_const_spec((n, d)), _const_spec((n, d))],
        out_shape=[jax.ShapeDtypeStruct((n, d), jnp.float32)] * 2,
        compiler_params=_params(1),
        name="conv_sample",
    )(x, st, w_in, taps, w_out, g, b)


def _ffn_split(f):
    assert f % MXU_DIM == 0
    cut = (f // MXU_DIM + 1) // 2 * MXU_DIM
    return (0, cut), (cut, f)


def _ffn_part(x, w_in_ref, w_out_ref, lo, hi):
    f = w_out_ref.shape[1]
    xb = _bf16(x)
    gate = jnp.dot(xb, w_in_ref[0, :, lo:hi], preferred_element_type=jnp.float32)
    up = jnp.dot(xb, w_in_ref[0, :, f + lo:f + hi], preferred_element_type=jnp.float32)
    a = gate * jax.nn.sigmoid(gate) * up
    return jnp.dot(_bf16(a), w_out_ref[0, lo:hi, :], preferred_element_type=jnp.float32)


def _ffn_sample_body(x_ref, w_in_ref, w_out_ref, g_ref, b_ref, o_ref):
    x = x_ref[...]
    y = _ffn_part(x, w_in_ref, w_out_ref, 0, w_out_ref.shape[1])
    o_ref[...] = _layernorm(ALPHA * x + y, g_ref[...], b_ref[...])


def _layer_spec(shape, layer):
    nd = len(shape)
    return pl.BlockSpec((1,) + shape[1:], lambda *_: (layer,) + (0,) * (nd - 1),
                        pipeline_mode=pl.Buffered(1))


def _ffn_sample(x, w_in, w_out, layer, g, b):
    n, d = x.shape
    return pl.pallas_call(
        _ffn_sample_body,
        grid=(1,),
        in_specs=[_const_spec(x.shape), _layer_spec(w_in.shape, layer),
                  _layer_spec(w_out.shape, layer), _const_spec(g.shape), _const_spec(b.shape)],
        out_specs=_const_spec((n, d)),
        out_shape=jax.ShapeDtypeStruct((n, d), jnp.float32),
        compiler_params=_params(1),
        name="ffn_sample",
    )(x, w_in, w_out, g, b)


def _rope_tables(pos):
    half = ROT_DIM // 2
    inv = jnp.power(ROPE_THETA, -jnp.arange(0, ROT_DIM, 2, dtype=jnp.float32) / ROT_DIM)
    ang = pos.astype(jnp.float32)[:, None] * inv[None, :]
    cos, sin = jnp.cos(ang), jnp.sin(ang)
    dd = jnp.arange(LANES) % HEAD_DIM
    cos_l = jnp.take(cos, dd % half, axis=1)
    sin_l = jnp.take(sin, dd % half, axis=1)
    c = jnp.where(dd[None, :] < ROT_DIM, cos_l, 1.0)
    s_up = jnp.where(dd[None, :] < half, -sin_l, 0.0)
    s_dn = jnp.where((dd[None, :] >= half) & (dd[None, :] < ROT_DIM), sin_l, 0.0)
    return c, s_up, s_dn


def _rope(x, c, s_up, s_dn):
    half = ROT_DIM // 2
    outs = []
    for g in range(x.shape[1] // LANES):
        xg = x[:, g * LANES:(g + 1) * LANES]
        x_up = pltpu.roll(xg, LANES - half, 1)
        x_dn = pltpu.roll(xg, half, 1)
        outs.append(xg * c + x_up * s_up + x_dn * s_dn)
    return jnp.concatenate(outs, axis=1)


def _qkv_rows(x_ref, w_ref, c_ref, su_ref, sd_ref, rows=slice(None)):
    x = x_ref[rows, :]
    d = x.shape[1]
    h3 = jnp.dot(_bf16(x), w_ref[...], preferred_element_type=jnp.float32)
    c, su, sd = c_ref[rows, :], su_ref[rows, :], sd_ref[rows, :]
    q = _rope(h3[:, :d], c, su, sd)
    k = _rope(h3[:, d:2 * d], c, su, sd)
    return q, k, h3[:, 2 * d:]


def _qkv_sample_body(x_ref, w_ref, c_ref, su_ref, sd_ref, q_ref, k_ref, v_ref):
    q, k, v = _qkv_rows(x_ref, w_ref, c_ref, su_ref, sd_ref)
    q_ref[...] = q * SCALE
    k_ref[...] = k
    v_ref[...] = v


def _qkv_sample(x, w, tables):
    n, d = x.shape
    specs = [_const_spec(a.shape) for a in (x, w) + tuple(tables)]
    return pl.pallas_call(
        _qkv_sample_body,
        grid=(1,),
        in_specs=specs,
        out_specs=[_const_spec((n, d))] * 3,
        out_shape=[jax.ShapeDtypeStruct((n, d), jnp.float32)] * 3,
        compiler_params=_params(1),
        name="qkv_sample",
    )(x, w, *tables)


def _qkv_prompt_body(x_ref, w_ref, c_ref, su_ref, sd_ref,
                     qt_ref, kb_ref, kt_ref, vt_ref, v_ref):
    ta = qt_ref.shape[2]
    for c in range(qt_ref.shape[0]):
        for sub in _sub_tiles(ta):
            rows = slice(c * ta + sub.start, c * ta + sub.stop)
            q, k, v = _qkv_rows(x_ref, w_ref, c_ref, su_ref, sd_ref, rows)
            qt_ref[c, :, sub] = _bf16((q * (SCALE * LOG2_E)).T)
            kb_ref[rows, :] = _bf16(k)
            kt_ref[0, :, rows] = k.T
            vt_ref[c, :, sub] = _bf16(v.T)
            v_ref[rows, :] = v


def _qkv_prompt(x, w, tables, seq_len):
    n, d = x.shape
    tm = TOKEN_TILE
    ta = ATTN_Q_TILE
    assert ATTN_Q_TILE == ATTN_KV_TILE and tm % ta == 0
    tps = seq_len // tm
    tspec = pl.BlockSpec((tm, LANES), lambda i: (i % tps, 0))
    row = pl.BlockSpec((tm, d), lambda i: (i, 0))
    blk = pl.BlockSpec((tm // ta, d, ta), lambda i: (i, 0, 0))
    return pl.pallas_call(
        _qkv_prompt_body,
        grid=(n // tm,),
        in_specs=[row, _const_spec(w.shape), tspec, tspec, tspec],
        out_specs=[blk, row, pl.BlockSpec((1, d, tm), lambda i: (i // tps, 0, i % tps)), blk, row],
        out_shape=[
            jax.ShapeDtypeStruct((n // ta, d, ta), jnp.bfloat16),
            jax.ShapeDtypeStruct((n, d), jnp.bfloat16),
            jax.ShapeDtypeStruct((n // seq_len, d, seq_len), jnp.float32),
            jax.ShapeDtypeStruct((n // ta, d, ta), jnp.bfloat16),
            jax.ShapeDtypeStruct((n, d), jnp.float32),
        ],
        compiler_params=_params(1),
        name="qkv_prompt",
    )(x, w, *tables)


def _lambda(lamv_ref):
    lv = lamv_ref[...]
    d1 = jnp.sum(lv[0:1, :] * lv[1:2, :], axis=-1, keepdims=True)
    d2 = jnp.sum(lv[2:3, :] * lv[3:4, :], axis=-1, keepdims=True)
    return jnp.exp(d1) - jnp.exp(d2) + LAM_INIT


def _subln(o, g):
    o = o * lax.rsqrt(jnp.mean(o * o, axis=-1, keepdims=True) + SUBLN_EPS)
    return o * g * (1.0 - LAM_INIT)


def _prompt_attn_body(qlo_ref, qhi_ref, k_ref, vt_ref, lamv_ref, gcol_ref, o_ref,
                      q_ref, s0_ref, s1_ref, bmax_ref, m_ref, acc_ref, *, n_tiles):
    tq, tk = ATTN_Q_TILE, ATTN_KV_TILE
    t_lo = pl.program_id(2)
    t_hi = n_tiles - 1 - t_lo
    neg = jnp.finfo(jnp.float32).min
    feat = lax.broadcasted_iota(jnp.int32, (2 * HEAD_DIM, 1), 0)
    for tile, ref in enumerate((qlo_ref, qhi_ref)):
        q_ref[tile, 0] = jnp.where(feat < HEAD_DIM, ref[0], 0)
        q_ref[tile, 1] = jnp.where(feat >= HEAD_DIM, ref[0], 0)
    ones = jnp.ones((acc_ref.shape[2] - V_DIM, tk), jnp.bfloat16)
    acc_ref[...] = jnp.zeros_like(acc_ref)
    m_ref[...] = jnp.full_like(m_ref, neg)
    s_refs = (s0_ref, s1_ref)

    def block(u):
        if u == 0:
            return 0, t_lo, True
        if u == 1:
            return 1, t_hi, True
        n = u - 2
        return jnp.where(n < t_lo, 0, 1), jnp.where(n < t_lo, n, n - t_lo), False

    def scores(u):
        tile, kv, masked = block(u)
        kblk = k_ref[0, pl.ds(pl.multiple_of(kv * tk, tk), tk), :]
        for c in range(2):
            st = jnp.dot(kblk, q_ref[tile, c], preferred_element_type=jnp.float32)
            if masked:
                key = lax.broadcasted_iota(jnp.int32, (tk, tq), 0)
                qry = lax.broadcasted_iota(jnp.int32, (tk, tq), 1)
                st = jnp.where(key <= qry, st, neg)
            s_refs[u % 2][c] = st
            bmax_ref[u % 2, c] = jnp.max(st, axis=0, keepdims=True)

    def accumulate(u):
        tile, kv, _ = block(u)
        s_ref = s_refs[u % 2]
        vext = jnp.concatenate([vt_ref[0, kv], ones], axis=0)
        for c in range(2):
            m_old = m_ref[tile, c]
            m_new = jnp.maximum(m_old, bmax_ref[u % 2, c])
            alpha = jnp.exp2(m_old - m_new)
            m_ref[tile, c] = m_new
            p = _bf16(jnp.exp2(s_ref[c] - m_new))
            acc_ref[tile, c] = alpha * acc_ref[tile, c] + jnp.dot(
                vext, p, preferred_element_type=jnp.float32)

    n_blocks = n_tiles + 1
    scores(0)
    for u in range(n_blocks):
        if u + 1 < n_blocks:
            scores(u + 1)
        accumulate(u)

    lam = _lambda(lamv_ref)
    for tile in range(2):
        o = (acc_ref[tile, 0, :V_DIM] / acc_ref[tile, 0, V_DIM:V_DIM + 1]
             - lam * (acc_ref[tile, 1, :V_DIM] / acc_ref[tile, 1, V_DIM:V_DIM + 1]))
        o = o * lax.rsqrt(jnp.mean(o * o, axis=0, keepdims=True) + SUBLN_EPS)
        o = o * gcol_ref[...] * (1.0 - LAM_INIT)
        o_ref[0, tile, 0] = o.T.astype(o_ref.dtype)


def _prompt_attn(qt, k, vt, lamv, gcol):
    b, s, d = k.shape
    tq, tk = ATTN_Q_TILE, ATTN_KV_TILE
    nq = s // tq
    assert tq == tk and s % tq == 0 and nq % 2 == 0
    return pl.pallas_call(
        functools.partial(_prompt_attn_body, n_tiles=nq),
        grid=(b, N_HEADS, nq // 2),
        in_specs=[
            pl.BlockSpec((1, V_DIM, tq), lambda bi, j, t: (bi * nq + t, j, 0)),
            pl.BlockSpec((1, V_DIM, tq), lambda bi, j, t: (bi * nq + nq - 1 - t, j, 0)),
            pl.BlockSpec((1, s, V_DIM), lambda bi, j, t: (bi, 0, j)),
            pl.BlockSpec((1, s // tk, V_DIM, tk), lambda bi, j, t: (bi, 0, j, 0)),
            _const_spec(lamv.shape), _const_spec(gcol.shape),
        ],
        out_specs=pl.BlockSpec((1, 2, 1, tq, V_DIM), lambda bi, j, t: (bi, 0, t, 0, j)),
        out_shape=jax.ShapeDtypeStruct((b, 2, nq // 2, tq, d), jnp.bfloat16),
        scratch_shapes=[
            pltpu.VMEM((2, 2, V_DIM, tq), jnp.bfloat16),
            pltpu.VMEM((2, tk, tq), jnp.float32),
            pltpu.VMEM((2, tk, tq), jnp.float32),
            pltpu.VMEM((2, 2, 1, tq), jnp.float32),
            pltpu.VMEM((2, 2, 1, tq), jnp.float32),
            pltpu.VMEM((2, 2, V_DIM + BF16_SUBLANES, tq), jnp.float32),
        ],
        compiler_params=_params(3),
        name="prompt_attn",
    )(qt, qt, k, vt, lamv, gcol)


def _paired_tile_slot(t, nq):
    return jnp.where(t < nq // 2, t, nq // 2 + nq - 1 - t)


class _DecodeRefs(NamedTuple):
    q: object
    kn: object
    vn: object
    lamv: object
    g: object
    k_pages: tuple
    v_pages: tuple
    out: object
    m: object
    l: object
    acc: object
    v3: object
    s: object
    alpha: object


def _decode_math(r):
    ds = r.q.shape[2]
    q3 = _bf16(r.q[0])

    def pair_rows(s):
        return s.reshape(N_HEADS, 2 * ds, s.shape[2])

    def scores(k3):
        return pair_rows(jnp.einsum("hqd,htd->hqt", q3, k3, preferred_element_type=jnp.float32))

    def scores_t(kt3):
        return pair_rows(jnp.einsum("hqd,hdt->hqt", q3, kt3, preferred_element_type=jnp.float32))

    def weighted(p, v3):
        return jnp.einsum("jrt,jte->jre", _bf16(p), v3, preferred_element_type=jnp.float32)

    return ds, scores, scores_t, weighted


def _decode_init(r):
    ds, scores, _, weighted = _decode_math(r)
    page = r.v_pages[0].shape[1]
    kn, vn = r.kn[0], r.vn[0]
    kn3 = _bf16(jnp.concatenate(
        [kn, jnp.zeros((kn.shape[0], page - ds, kn.shape[2]), jnp.float32)], axis=1))
    vn3 = _bf16(jnp.concatenate(
        [vn, jnp.zeros((vn.shape[0], page - ds, vn.shape[2]), jnp.float32)], axis=1))
    s = scores(kn3)
    qi = lax.broadcasted_iota(jnp.int32, s.shape, 1) % ds
    tt = lax.broadcasted_iota(jnp.int32, s.shape, 2)
    s = jnp.where(tt <= qi, s, jnp.finfo(jnp.float32).min)
    m = jnp.max(s, axis=-1, keepdims=True)
    p = jnp.exp(s - m)
    r.m[...] = m
    r.l[...] = jnp.sum(p, axis=-1, keepdims=True)
    r.acc[...] = weighted(p, vn3)


def _head_rows(tokens):
    return tokens // 2 + SUBLANES


def _decode_values(r):
    page = r.v_pages[0].shape[1]
    stride = _head_rows(page * len(r.v_pages))
    for n, ref in enumerate(r.v_pages):
        for t in range(0, page, 2):
            words = pltpu.pack_elementwise([ref[0, t], ref[0, t + 1]],
                                           packed_dtype=jnp.bfloat16)
            r.v3[pl.ds((n * page + t) // 2, N_HEADS, stride=stride), :] = words


def _decode_scores(r):
    _, _, scores_t, _ = _decode_math(r)
    r.s[...] = scores_t(jnp.concatenate([_bf16(ref[0]) for ref in r.k_pages], axis=2))


def _decode_softmax(r):
    s = r.s[...]
    m_old = r.m[...]
    m_new = jnp.maximum(m_old, jnp.max(s, axis=-1, keepdims=True))
    alpha = jnp.exp(m_old - m_new)
    p = jnp.exp(s - m_new)
    r.l[...] = alpha * r.l[...] + jnp.sum(p, axis=-1, keepdims=True)
    r.m[...] = m_new
    r.alpha[...] = alpha
    r.s[...] = p


def _decode_accumulate(r):
    _, _, _, weighted = _decode_math(r)
    tokens = r.s.shape[2]
    stride = _head_rows(tokens)
    v3 = jnp.stack([pltpu.bitcast(r.v3[j * stride:j * stride + tokens // 2, :], jnp.bfloat16)
                    for j in range(N_HEADS)])
    r.acc[...] = r.alpha[...] * r.acc[...] + weighted(r.s[...], v3)


def _decode_finish(r):
    ds = r.q.shape[2]
    o = r.acc[...] / r.l[...]
    o = o[:, :ds, :] - _lambda(r.lamv) * o[:, ds:, :]
    o = _subln(o, r.g[...])
    for j in range(N_HEADS):
        r.out[0, :, j * V_DIM:(j + 1) * V_DIM] = o[j]


def _ffn_decode_body(pt_ref, x_ref, w_in_ref, w_out_ref, g_ref, b_ref,
                     q_ref, kn_ref, vn_ref, lamv_ref, gsub_ref, *refs, chunks_per_seq):
    pps = PAGES_PER_STEP
    kc_ref, vc_ref, o_ref, od_ref, y_ref = refs[:5]
    kbuf_ref, vbuf_ref, sem_ref = refs[-3:]
    s = pl.program_id(0)
    n_steps = pl.num_programs(0)
    half = s % 2
    chunk = s % chunks_per_seq
    slot = s % PAGE_SLOTS

    def page_copies(step, to_slot):
        base = step // chunks_per_seq * (chunks_per_seq * pps) + step % chunks_per_seq * pps
        out = []
        for pg in range(pps):
            pid = pt_ref[base + pg]
            out.append(pltpu.make_async_copy(kc_ref.at[pid], kbuf_ref.at[to_slot, pg],
                                             sem_ref.at[to_slot]))
            out.append(pltpu.make_async_copy(vc_ref.at[pid], vbuf_ref.at[to_slot, pg],
                                             sem_ref.at[to_slot]))
        return out

    @pl.when(s == 0)
    def _():
        for step in range(PAGE_SLOTS - 1):
            for cp in page_copies(step, step):
                cp.start()

    @pl.when(s + PAGE_SLOTS - 1 < n_steps)
    def _():
        ahead = s + PAGE_SLOTS - 1
        for cp in page_copies(ahead, ahead % PAGE_SLOTS):
            cp.start()

    for cp in page_copies(s, slot):
        cp.wait()

    dec = _DecodeRefs(q_ref, kn_ref, vn_ref, lamv_ref, gsub_ref,
                      tuple(kbuf_ref.at[slot, pl.ds(pg, 1)] for pg in range(pps)),
                      tuple(vbuf_ref.at[slot, pl.ds(pg, 1)] for pg in range(pps)),
                      od_ref, *refs[5:-3])

    @pl.when(chunk == 0)
    def _():
        _decode_init(dec)

    def step(h_static):
        x = x_ref[...]
        _decode_values(dec)
        _decode_scores(dec)
        y = _ffn_part(x, w_in_ref, w_out_ref, *_ffn_split(w_out_ref.shape[1])[h_static])
        _decode_softmax(dec)
        _decode_accumulate(dec)
        if h_static == 0:
            y_ref[...] = y
        else:
            o_ref[...] = _layernorm(ALPHA * x + (y_ref[...] + y), g_ref[...], b_ref[...])

    for h_static in range(2):
        pl.when(half == h_static)(functools.partial(step, h_static))

    @pl.when(chunk == chunks_per_seq - 1)
    def _():
        _decode_finish(dec)


def _ffn_decode(x, w_in, w_out, layer, g, b,
                q3, k_new3, v_new3, cache_kt, cache_v, page_table, lamv, gsub):
    n, d = x.shape
    sb, nh2, ds, hd = q3.shape
    n_pages = page_table.shape[1]
    page = cache_v.shape[1]
    pps = PAGES_PER_STEP
    chunks = n_pages // pps
    rows = SUB_TILE
    n_steps = 2 * (n // rows)
    assert n_pages % pps == 0 and ds == SUBLANES and n_steps == sb * chunks

    def page_spec(shape, pg):
        return pl.BlockSpec(
            (1,) + shape[1:],
            lambda s, pt: (pt[s // chunks * n_pages + s % chunks * pps + pg], 0, 0, 0))

    def seq_spec(shape):
        return pl.BlockSpec((1,) + shape[1:],
                            lambda s, pt: (s // chunks,) + (0,) * (len(shape) - 1))

    row_spec = pl.BlockSpec((rows, d), lambda s, pt: (s // 2, 0))
    grid_spec = pltpu.PrefetchScalarGridSpec(
        num_scalar_prefetch=1,
        grid=(n_steps,),
        in_specs=[row_spec, _layer_spec(w_in.shape, layer), _layer_spec(w_out.shape, layer),
                  _const_spec(g.shape), _const_spec(b.shape),
                  seq_spec(q3.shape), seq_spec(k_new3.shape), seq_spec(v_new3.shape),
                  _const_spec(lamv.shape), _const_spec(gsub.shape),
                  pl.BlockSpec(memory_space=pl.ANY), pl.BlockSpec(memory_space=pl.ANY)],
        out_specs=[row_spec, seq_spec((sb, ds, d))],
        scratch_shapes=[
            pltpu.VMEM((rows, d), jnp.float32),
            pltpu.VMEM((N_HEADS, 2 * ds, 1), jnp.float32),
            pltpu.VMEM((N_HEADS, 2 * ds, 1), jnp.float32),
            pltpu.VMEM((N_HEADS, 2 * ds, V_DIM), jnp.float32),
            pltpu.VMEM((N_HEADS * _head_rows(pps * page), V_DIM), jnp.uint32),
            pltpu.VMEM((N_HEADS, 2 * ds, pps * page), jnp.float32),
            pltpu.VMEM((N_HEADS, 2 * ds, 1), jnp.float32),
            pltpu.VMEM((PAGE_SLOTS, pps) + cache_kt.shape[1:], jnp.float32),
            pltpu.VMEM((PAGE_SLOTS, pps) + cache_v.shape[1:], jnp.float32),
            pltpu.SemaphoreType.DMA((PAGE_SLOTS,)),
        ],
    )
    return pl.pallas_call(
        functools.partial(_ffn_decode_body, chunks_per_seq=chunks),
        grid_spec=grid_spec,
        out_shape=[jax.ShapeDtypeStruct((n, d), jnp.float32),
                   jax.ShapeDtypeStruct((sb, ds, d), jnp.float32)],
        compiler_params=_params(1),
        name="ffn_decode",
    )(page_table.reshape(-1), x, w_in, w_out, g, b, q3, k_new3, v_new3, lamv, gsub,
      cache_kt, cache_v)


def _proj_ln_body(*refs):
    o_refs, (x_ref, w_ref, g_ref, b_ref, y_ref) = refs[:-5], refs[-5:]
    to = o_refs[0].shape[0]
    for n, o_ref in enumerate(o_refs):
        for sub in _sub_tiles(to):
            rows = slice(n * to + sub.start, n * to + sub.stop)
            y = jnp.dot(_bf16(o_ref[sub, :]), w_ref[...], preferred_element_type=jnp.float32)
            y_ref[rows, :] = _layernorm(ALPHA * x_ref[rows, :] + y, g_ref[...], b_ref[...])


def _proj_ln(o, x, w, g, b, tm, to, o_tile=lambda i: i):
    n, d = x.shape
    assert n % tm == 0 and tm % to == 0
    per = tm // to
    row = pl.BlockSpec((tm, d), lambda i: (i, 0))
    o_specs = [pl.BlockSpec((to, d), lambda i, k=k: (o_tile(i * per + k), 0)) for k in range(per)]
    return pl.pallas_call(
        _proj_ln_body,
        grid=(n // tm,),
        in_specs=o_specs + [row, _const_spec(w.shape), _const_spec(g.shape), _const_spec(b.shape)],
        out_specs=row,
        out_shape=jax.ShapeDtypeStruct((n, d), jnp.float32),
        compiler_params=_params(1),
        name="proj_ln",
    )(*([o] * per), x, w, g, b)


def kernel(x_prompt, x_sample, state_conv, cache_k, cache_v, page_table, w_conv_in, w_conv, w_conv_out, w_qkv, lambda_q1, lambda_k1, lambda_q2, lambda_k2, subln_g, w_attn_out, ln_mix_g, ln_mix_b, w_ffn_in, w_ffn_out, ln_ffn_g, ln_ffn_b):
    b, s, d = x_prompt.shape
    db, ds, _ = x_sample.shape
    n_pool, page = cache_k.shape[:2]
    past_len = page_table.shape[1] * page
    f32 = jnp.float32
    assert ds == SUBLANES and s % TOKEN_TILE == 0

    w_conv_in_b, w_conv_out_b = _bf16(w_conv_in), _bf16(w_conv_out)
    w_qkv_b, w_attn_out_b = _bf16(w_qkv), _bf16(w_attn_out)
    w_ffn_in_b, w_ffn_out_b = _bf16(w_ffn_in), _bf16(w_ffn_out)
    taps = w_conv.astype(f32)
    row = lambda a: a.reshape(1, -1).astype(f32)
    lamv = jnp.stack([lambda_q1, lambda_k1, lambda_q2, lambda_k2]).astype(f32)
    g_sub = row(subln_g)

    xp = x_prompt.reshape(b * s, d)
    xs = x_sample.reshape(db * ds, d)

    i = 0
    xp, tail_p = _conv_prompt(xp, w_conv_in_b, taps, w_conv_out_b,
                              row(ln_mix_g[i]), row(ln_mix_b[i]), s)
    conv_p = tail_p.reshape(b, SUBLANES, d)[:, SUBLANES - (CONV_WIDTH - 1):]
    st = jnp.pad(state_conv, ((0, 0), (0, ds - (CONV_WIDTH - 1)), (0, 0))).reshape(db * ds, d)
    xs, u_s = _conv_sample(xs, st, w_conv_in_b, taps, w_conv_out_b,
                           row(ln_mix_g[i]), row(ln_mix_b[i]))
    conv_s = u_s.reshape(db, ds, d)[:, ds - (CONV_WIDTH - 1):]
    xs = _ffn_sample(xs, w_ffn_in_b, w_ffn_out_b, i, row(ln_ffn_g[i]), row(ln_ffn_b[i]))

    tab_s = _rope_tables(past_len + jnp.arange(db * ds) % ds)
    qs, k_s, v_s = _qkv_sample(xs, w_qkv_b, tab_s)
    heads_first = lambda a, nh: a.reshape(db, ds, nh, d // nh).transpose(0, 2, 1, 3)
    q3, k_new3, v_new3 = (heads_first(qs, 2 * N_HEADS), heads_first(k_s, 2 * N_HEADS),
                          heads_first(v_s, N_HEADS))
    cache_kt = cache_k.transpose(0, 2, 3, 1)
    hb = db // 2

    def ffn_decode(x, layer, seqs):
        return _ffn_decode(x, w_ffn_in_b, w_ffn_out_b, layer,
                           row(ln_ffn_g[layer]), row(ln_ffn_b[layer]),
                           q3[seqs], k_new3[seqs], v_new3[seqs], cache_kt, cache_v,
                           page_table[seqs], lamv, g_sub)

    xp, os_lo = ffn_decode(xp, i, slice(0, hb))

    i = 1
    tab_p = _rope_tables(jnp.arange(s))
    qt_p, kb_p, kt_p, vt_p, v_p = _qkv_prompt(xp, w_qkv_b, tab_p, s)
    k_p = kt_p.reshape(b, 2 * N_HEADS, HEAD_DIM, s).transpose(0, 3, 1, 2)

    op = _prompt_attn(qt_p, kb_p.reshape(b, s, d),
                      vt_p.reshape(b, s // ATTN_KV_TILE, d, ATTN_KV_TILE),
                      lamv, g_sub.reshape(-1, 1))
    nq = s // ATTN_Q_TILE
    xp = _proj_ln(op.reshape(b * s, d), xp, w_attn_out_b, row(ln_mix_g[i]), row(ln_mix_b[i]),
                  tm=TOKEN_TILE, to=ATTN_Q_TILE,
                  o_tile=lambda r: r // nq * nq + _paired_tile_slot(r % nq, nq))
    xp, os_hi = ffn_decode(xp, i, slice(hb, db))
    os_ = jnp.concatenate([os_lo, os_hi], axis=0)
    xs = _proj_ln(os_.reshape(db * ds, d), xs, w_attn_out_b, row(ln_mix_g[i]), row(ln_mix_b[i]),
                  tm=db * ds, to=db * ds)
    xs = _ffn_sample(xs, w_ffn_in_b, w_ffn_out_b, i, row(ln_ffn_g[i]), row(ln_ffn_b[i]))

    return (xp.reshape(b, s, d), xs.reshape(db, ds, d), conv_p,
            k_p, v_p.reshape(b, s, N_HEADS, V_DIM),
            conv_s,
            k_s.reshape(db, ds, 2 * N_HEADS, HEAD_DIM), v_s.reshape(db, ds, N_HEADS, V_DIM))
```
